```python
import math
import jax, jax.numpy as jnp
from jax import lax
import numpy as np

D_MODEL = 1024
BATCH = 8
SEQ = 8192
DEPTH = 1

ATT_HEADS = 8
HEAD_DIM = 64
ATT_WIDTH = ATT_HEADS * HEAD_DIM
ATT_UP_DIM = D_MODEL // ATT_HEADS
BLOCK_Q = 128
POOL_WINDOWS = (2, 4, 8, 16)
POOL_GROUPS = len(POOL_WINDOWS)
POOL_GROUP_DIM = 128
POOL_WIDTH = POOL_GROUPS * POOL_GROUP_DIM
POOL_UP_DIM = D_MODEL // POOL_GROUPS
IN_SPLITS = (ATT_WIDTH, 2 * ATT_WIDTH, 3 * ATT_WIDTH, 3 * ATT_WIDTH + POOL_WIDTH,
             3 * ATT_WIDTH + POOL_WIDTH + D_MODEL)
IN_WIDTH = 3 * ATT_WIDTH + POOL_WIDTH + 2 * D_MODEL
N_KEYS = 128
N_EXPERTS = N_KEYS * N_KEYS
PEER_HEADS = 8
PEER_QUERY_DIM = 256
PEER_HALF_DIM = PEER_QUERY_DIM // 2
PEER_TOPK = 16
PEER_CHUNK = 128
PLE_DIM = 256
EPS = 1e-6

kernel_name = "hybrid_stickbreak_pool_peer_block"


def rmsnorm(x, g):
    xf = x.astype(jnp.float32)
    xf = xf * lax.rsqrt(jnp.mean(xf * xf, axis=-1, keepdims=True) + EPS)
    return xf.astype(x.dtype) * g


def stick_breaking_attention(q, k, v):
    B, S, H, dh = q.shape
    nb = S // BLOCK_Q
    scale = dh ** -0.5
    qb = q.reshape(B, nb, BLOCK_Q, H, dh).transpose(1, 0, 3, 2, 4)
    kt = k.transpose(0, 2, 1, 3)
    vt = v.transpose(0, 2, 1, 3)
    offs = jnp.arange(BLOCK_Q, dtype=jnp.int32)

    def one_query_block(args):
        i, q_blk = args
        q_pos = i * BLOCK_Q + offs

        def body(n, carry):
            acc, logsurv = carry
            j = i - n
            k_blk = lax.dynamic_slice_in_dim(kt, j * BLOCK_Q, BLOCK_Q, axis=2)
            v_blk = lax.dynamic_slice_in_dim(vt, j * BLOCK_Q, BLOCK_Q, axis=2)
            z = jnp.einsum('bhqd,bhkd->bhqk', q_blk, k_blk).astype(jnp.float32) * scale
            k_pos = j * BLOCK_Q + offs
            mask = k_pos[None, :] < q_pos[:, None]
            log_1m = jnp.where(mask, jax.nn.log_sigmoid(-z), 0.0)
            later = lax.cumsum(log_1m, axis=3, reverse=True) - log_1m
            w = jnp.where(mask,
                          jnp.exp(jax.nn.log_sigmoid(z) + later + logsurv[..., None]),
                          0.0)
            acc = acc + jnp.einsum('bhqk,bhkd->bhqd', w, v_blk.astype(jnp.float32))
            return acc, logsurv + jnp.sum(log_1m, axis=-1)

        init = (jnp.zeros((B, H, BLOCK_Q, dh), jnp.float32),
                jnp.zeros((B, H, BLOCK_Q), jnp.float32))
        acc, _ = lax.fori_loop(0, i + 1, body, init)
        return acc

    out = lax.map(one_query_block, (jnp.arange(nb, dtype=jnp.int32), qb))
    return out.transpose(1, 0, 3, 2, 4).reshape(B, S, H, dh).astype(q.dtype)


def multiscale_pool(u):
    B, S, C = u.shape
    uf = u.astype(jnp.float32)
    c0 = jnp.concatenate([jnp.zeros((B, 1, C), jnp.float32), jnp.cumsum(uf, axis=1)], axis=1)
    t = jnp.arange(1, S + 1, dtype=jnp.float32)
    outs = []
    for g, w in enumerate(POOL_WINDOWS):
        cg = c0[..., g * POOL_GROUP_DIM:(g + 1) * POOL_GROUP_DIM]
        lower = jnp.concatenate([jnp.zeros((B, w, POOL_GROUP_DIM), jnp.float32),
                                 cg[:, :S + 1 - w]], axis=1)
        window_sum = (cg - lower)[:, 1:]
        outs.append(window_sum / jnp.minimum(t, float(w))[None, :, None])
    pooled = jnp.concatenate(outs, axis=-1)
    return (pooled - uf).astype(u.dtype)


def peer(h, w_query, sub_keys, expert_u, expert_v):
    B, S, D = h.shape
    hc = h.reshape(-1, PEER_CHUNK, D)

    def one_chunk(h_c):
        q = (h_c @ w_query).reshape(PEER_CHUNK, PEER_HEADS, 2, PEER_HALF_DIM)
        scores = jnp.einsum('thcd,hcnd->thcn', q, sub_keys).astype(jnp.float32)
        s_top, i_top = lax.top_k(scores, PEER_TOPK)
        cand_s = s_top[:, :, 0, :, None] + s_top[:, :, 1, None, :]
        cand_i = i_top[:, :, 0, :, None] * N_KEYS + i_top[:, :, 1, None, :]
        best_s, best_pos = lax.top_k(cand_s.reshape(PEER_CHUNK, PEER_HEADS, -1), PEER_TOPK)
        idx = jnp.take_along_axis(cand_i.reshape(PEER_CHUNK, PEER_HEADS, -1), best_pos, axis=-1)
        gates = jax.nn.softmax(best_s, axis=-1)
        u = jnp.take(expert_u, idx, axis=0)
        act = jax.nn.gelu(jnp.einsum('thkd,td->thk', u, h_c).astype(jnp.float32),
                          approximate=False)
        v = jnp.take(expert_v, idx, axis=0)
        return jnp.einsum('thk,thkd->td', (gates * act).astype(h.dtype), v)

    return lax.map(one_chunk, hc).reshape(B, S, D)


def setup_inputs(seed: int = 0) -> dict:
    key = jax.random.key(seed)
    ks = jax.random.split(key, 20)
    f32 = jnp.float32

    def nrm(k, shape, scale):
        return jax.random.normal(k, shape, f32) * scale

    def gain(k, shape):
        return 1.0 + 0.02 * jax.random.normal(k, shape, f32)

    return {
        "x": nrm(ks[0], (BATCH, SEQ, D_MODEL), 1.0),
        "p": nrm(ks[1], (DEPTH, BATCH, SEQ, PLE_DIM), 1.0),
        "norm_mix": gain(ks[2], (DEPTH, D_MODEL)),
        "w_in": nrm(ks[3], (DEPTH, D_MODEL, IN_WIDTH), D_MODEL ** -0.5),
        "w_att_up": nrm(ks[4], (DEPTH, ATT_HEADS, HEAD_DIM, ATT_UP_DIM), HEAD_DIM ** -0.5),
        "w_pool_group": nrm(ks[5], (DEPTH, POOL_GROUPS, POOL_GROUP_DIM, POOL_UP_DIM),
                            POOL_GROUP_DIM ** -0.5),
        "pool_scale": gain(ks[6], (DEPTH, D_MODEL)),
        "w_out": nrm(ks[7], (DEPTH, D_MODEL, D_MODEL), D_MODEL ** -0.5),
        "norm_ffn": gain(ks[8], (DEPTH, D_MODEL)),
        "w_query": nrm(ks[9], (DEPTH, D_MODEL, PEER_HEADS * PEER_QUERY_DIM), D_MODEL ** -0.5),
        "sub_keys": nrm(ks[10], (DEPTH, PEER_HEADS, 2, N_KEYS, PEER_HALF_DIM),
                        PEER_HALF_DIM ** -0.5),
        "expert_u": nrm(ks[11], (DEPTH, N_EXPERTS, D_MODEL), D_MODEL ** -0.5),
        "expert_v": nrm(ks[12], (DEPTH, N_EXPERTS, D_MODEL), 0.5),
        "norm_ple": gain(ks[13], (DEPTH, D_MODEL)),
        "w_ple_gate": nrm(ks[14], (DEPTH, D_MODEL, D_MODEL), D_MODEL ** -0.5),
        "w_ple": nrm(ks[15], (DEPTH, PLE_DIM, D_MODEL), PLE_DIM ** -0.5),
        "norm_final": gain(ks[16], (D_MODEL,)),
    }


def reference(x, p, norm_mix, w_in, w_att_up, w_pool_group, pool_scale, w_out,
              norm_ffn, w_query, sub_keys, expert_u, expert_v, norm_ple, w_ple_gate,
              w_ple, norm_final):
    B, S, _ = x.shape
    for i in range(DEPTH):
        h = rmsnorm(x, norm_mix[i])
        proj = h @ w_in[i]
        q, k, v, u_pool, g_att, g_pool = jnp.split(proj, IN_SPLITS, axis=-1)
        q = q.reshape(B, S, ATT_HEADS, HEAD_DIM)
        k = k.reshape(B, S, ATT_HEADS, HEAD_DIM)
        v = v.reshape(B, S, ATT_HEADS, HEAD_DIM)
        y_att = stick_breaking_attention(q, k, v)
        a_up = jnp.einsum('bshd,hde->bshe', y_att, w_att_up[i]).reshape(B, S, D_MODEL)
        y_pool = multiscale_pool(u_pool).reshape(B, S, POOL_GROUPS, POOL_GROUP_DIM)
        p_up = jnp.einsum('bsgc,gce->bsge', y_pool, w_pool_group[i]).reshape(B, S, D_MODEL)
        p_up = p_up * pool_scale[i]
        merged = jax.nn.sigmoid(g_att) * a_up + jax.nn.sigmoid(g_pool) * p_up
        x = x + merged @ w_out[i]
        x = x + peer(rmsnorm(x, norm_ffn[i]), w_query[i], sub_keys[i], expert_u[i], expert_v[i])
        gate = jax.nn.sigmoid(rmsnorm(x, norm_ple[i]) @ w_ple_gate[i])
        x = x + gate * (p[i] @ w_ple[i])
    return rmsnorm(x, norm_final)
```

```python
import functools
import math

import jax
import jax.numpy as jnp
from jax import lax
from jax.experimental import pallas as pl
from jax.experimental.pallas import tpu as pltpu

F32 = jnp.float32
BF16 = jnp.bfloat16

EPS = 1e-6
ATT_BLOCK = 128
POOL_WINDOWS = (2, 4, 8, 16)
PEER_TOPK = 16
N_KEYS = 128
V7X_VMEM_BYTES = 64 * 1024 * 1024
VMEM_LIMIT = V7X_VMEM_BYTES - 8 * 1024 * 1024
GATHER_TOKENS = 64
TILE_STRIDE = 136
NEG_INF = float("-inf")


def _cparams(n_axes):
    return pltpu.CompilerParams(
        dimension_semantics=("arbitrary",) * n_axes, vmem_limit_bytes=VMEM_LIMIT)


def _rms(x):
    return x * lax.rsqrt(jnp.mean(x * x, axis=-1, keepdims=True) + EPS)


def _dot(a, b):
    return jnp.dot(a, b, preferred_element_type=F32)


def _dot_nt(a, b):
    return lax.dot_general(a, b, (((1,), (1,)), ((), ())), preferred_element_type=F32)


def _split_bf16(x):
    hi = x.astype(BF16)
    lo = (x - hi.astype(F32)).astype(BF16)
    return hi, lo


def _in_proj_kernel(x_ref, g_ref, w_ref, q_ref, k_ref, v_ref, u_ref, ga_ref, gp_ref,
                    *, heads, head_dim):
    h = (_rms(x_ref[...]) * g_ref[...]).astype(BF16)
    att = heads * head_dim
    scale = head_dim ** -0.5

    def proj(lo, width):
        return _dot(h, w_ref[:, lo:lo + width])

    pq, pk, pv = proj(0, att), proj(att, att), proj(2 * att, att)
    for hd in range(heads):
        sl = slice(hd * head_dim, (hd + 1) * head_dim)
        q_ref[hd] = (pq[:, sl] * scale).astype(BF16)
        k_ref[hd] = pk[:, sl].astype(BF16)
        v_ref[hd] = pv[:, sl].astype(BF16)
    pool = u_ref.shape[1]
    d = ga_ref.shape[1]
    u_ref[...] = proj(3 * att, pool).astype(BF16)
    ga_ref[...] = proj(3 * att + pool, d).astype(BF16)
    gp_ref[...] = proj(3 * att + pool + d, d).astype(BF16)


def _in_proj(x2, g, w_bf, heads, head_dim, pool, tm=512):
    t, d = x2.shape
    n = w_bf.shape[1]
    kern = functools.partial(_in_proj_kernel, heads=heads, head_dim=head_dim)
    hshape = jax.ShapeDtypeStruct((heads, t, head_dim), BF16)
    hspec = pl.BlockSpec((heads, tm, head_dim), lambda i: (0, i, 0))
    return pl.pallas_call(
        kern,
        out_shape=(hshape, hshape, hshape,
                   jax.ShapeDtypeStruct((t, pool), BF16),
                   jax.ShapeDtypeStruct((t, d), BF16),
                   jax.ShapeDtypeStruct((t, d), BF16)),
        grid=(t // tm,),
        in_specs=[pl.BlockSpec((tm, d), lambda i: (i, 0)),
                  pl.BlockSpec((1, d), lambda i: (0, 0)),
                  pl.BlockSpec((d, n), lambda i: (0, 0))],
        out_specs=(hspec, hspec, hspec,
                   pl.BlockSpec((tm, pool), lambda i: (i, 0)),
                   pl.BlockSpec((tm, d), lambda i: (i, 0)),
                   pl.BlockSpec((tm, d), lambda i: (i, 0))),
        compiler_params=_cparams(1),
        name="in_proj",
    )(x2, g.reshape(1, d), w_bf)


def _attn_kernel(q_ref, k_ref, v_ref, o_ref):
    blk = ATT_BLOCK
    i = pl.program_id(2)
    q = q_ref[0]
    row = lax.broadcasted_iota(jnp.int32, (blk, blk), 0)
    col = lax.broadcasted_iota(jnp.int32, (blk, blk), 1)
    col2 = lax.broadcasted_iota(jnp.int32, (blk, 2 * blk), 1)
    row2 = lax.broadcasted_iota(jnp.int32, (blk, 2 * blk), 0)
    later_and_ones = jnp.where((row2 > col2) | (col2 >= blk), 1.0, 0.0).astype(BF16)

    def block(j, acc, logsurv, mask):
        start = pl.multiple_of(j * blk, blk)
        kb = k_ref[0, pl.ds(start, blk), :]
        vb = v_ref[0, pl.ds(start, blk), :]
        z = _dot_nt(q, kb)
        log_1m = -(jnp.maximum(z, 0.0) + jnp.log(1.0 + jnp.exp(-jnp.abs(z))))
        log_beta = z + log_1m
        if mask is not None:
            log_1m = jnp.where(mask, log_1m, 0.0)
        hi, lo = _split_bf16(log_1m)
        sums = _dot(jnp.concatenate([hi, lo], axis=0), later_and_ones)
        later = sums[:blk, :blk] + sums[blk:, :blk]
        total = sums[:blk, blk:] + sums[blk:, blk:]
        w = jnp.exp(log_beta + later + logsurv)
        if mask is not None:
            w = jnp.where(mask, w, 0.0)
        acc = acc + _dot(w.astype(BF16), vb)
        return acc, logsurv + total

    acc0 = jnp.zeros((blk, q.shape[1]), F32)
    surv0 = jnp.zeros((blk, blk), F32)
    acc, logsurv = block(i, acc0, surv0, col < row)

    def body(n, carry):
        return block(i - 1 - n, carry[0], carry[1], None)

    acc, _ = lax.fori_loop(0, i, body, (acc, logsurv))
    o_ref[0] = acc.astype(o_ref.dtype)


def _attention(q, k, v, batch, seq):
    heads, t, dh = q.shape
    nq = seq // ATT_BLOCK
    qspec = pl.BlockSpec((1, ATT_BLOCK, dh), lambda b, h, i: (h, b * nq + i, 0))
    kvspec = pl.BlockSpec((1, seq, dh), lambda b, h, i: (h, b, 0))
    return pl.pallas_call(
        _attn_kernel,
        out_shape=jax.ShapeDtypeStruct((heads, t, dh), BF16),
        grid=(batch, heads, nq),
        in_specs=[qspec, kvspec, kvspec],
        out_specs=qspec,
        compiler_params=_cparams(3),
        name="attn",
    )(q, k, v)


def _mix_kernel(x_ref, att_ref, u_ref, ga_ref, gp_ref, wa_ref, wp_ref, ps_ref, wo_ref,
                gn_ref, x1_ref, hn_ref, prev_ref, *, tm):
    s = pl.program_id(1)

    @pl.when(s == 0)
    def _():
        prev_ref[...] = jnp.zeros_like(prev_ref)

    heads = att_ref.shape[0]
    a_up = jnp.concatenate(
        [_dot(att_ref[hd], wa_ref[hd]) for hd in range(heads)], axis=1)

    u = u_ref[...]
    ext = jnp.concatenate([prev_ref[...], u], axis=0)
    prev_ref[...] = u
    r = lax.broadcasted_iota(jnp.int32, (tm, 2 * tm), 0) + tm
    c = lax.broadcasted_iota(jnp.int32, (tm, 2 * tm), 1)
    pos = (s * tm + lax.broadcasted_iota(jnp.int32, (tm, 1), 0) + 1).astype(F32)
    groups = len(POOL_WINDOWS)
    gdim = u.shape[1] // groups
    uf = u.astype(F32)
    p_parts = []
    for g, w in enumerate(POOL_WINDOWS):
        band = jnp.where((c <= r) & (c > r - w), 1.0, 0.0).astype(BF16)
        sl = slice(g * gdim, (g + 1) * gdim)
        window_sum = _dot(band, ext[:, sl])
        y = window_sum / jnp.minimum(pos, float(w)) - uf[:, sl]
        p_parts.append(_dot(y.astype(BF16), wp_ref[g]))
    p_up = jnp.concatenate(p_parts, axis=1) * ps_ref[...]

    merged = (jax.nn.sigmoid(ga_ref[...].astype(F32)) * a_up
              + jax.nn.sigmoid(gp_ref[...].astype(F32)) * p_up)
    x1 = x_ref[...] + _dot(merged.astype(BF16), wo_ref[...])
    x1_ref[...] = x1
    hn_ref[...] = (_rms(x1) * gn_ref[...]).astype(BF16)


def _mix(x2, att, u, ga, gp, wa, wp, ps, wo, gn, batch, seq, tm=256):
    t, d = x2.shape
    heads, _, dh = att.shape
    pool = u.shape[1]
    ns = seq // tm
    tok = lambda b, s: (b * ns + s, 0)
    const2 = lambda b, s: (0, 0)
    const3 = lambda b, s: (0, 0, 0)
    kern = functools.partial(_mix_kernel, tm=tm)
    return pl.pallas_call(
        kern,
        out_shape=(jax.ShapeDtypeStruct((t, d), F32), jax.ShapeDtypeStruct((t, d), BF16)),
        grid=(batch, ns),
        in_specs=[pl.BlockSpec((tm, d), tok),
                  pl.BlockSpec((heads, tm, dh), lambda b, s: (0, b * ns + s, 0)),
                  pl.BlockSpec((tm, pool), tok),
                  pl.BlockSpec((tm, d), tok),
                  pl.BlockSpec((tm, d), tok),
                  pl.BlockSpec(wa.shape, const3),
                  pl.BlockSpec(wp.shape, const3),
                  pl.BlockSpec((1, d), const2),
                  pl.BlockSpec((d, d), const2),
                  pl.BlockSpec((1, d), const2)],
        out_specs=(pl.BlockSpec((tm, d), tok), pl.BlockSpec((tm, d), tok)),
        scratch_shapes=[pltpu.VMEM((tm, pool), BF16)],
        compiler_params=_cparams(2),
        name="mix",
    )(x2, att, u, ga, gp, wa, wp, ps.reshape(1, d), wo, gn.reshape(1, d))


def _topk_rows(s, payload, k):
    rows = s.shape[0]
    pos = lax.broadcasted_iota(jnp.int32, s.shape, 0).astype(F32)
    vals, poss, pays = [], [], []
    for _ in range(k):
        m = jnp.max(s, axis=0, keepdims=True)
        p = jnp.min(jnp.where(s == m, pos, float(rows)), axis=0, keepdims=True)
        hit = pos == p
        if payload is not None:
            pays.append(jnp.sum(jnp.where(hit, payload, 0.0), axis=0, keepdims=True))
        s = jnp.where(hit, NEG_INF, s)
        vals.append(m)
        poss.append(p)
    cat = lambda xs: jnp.concatenate(xs, axis=0)
    return cat(vals), cat(poss), (cat(pays) if payload is not None else None)


def _route_kernel(hn_ref, wq_ref, keys_ref, idx_ref, gate_ref, *, heads):
    q = _dot(hn_ref[...], wq_ref[...]).astype(BF16)
    half = keys_ref.shape[-1]
    k = PEER_TOPK
    for hd in range(heads):
        tops = []
        for c in range(2):
            lo = (hd * 2 + c) * half
            scores = _dot_nt(keys_ref[hd, c], q[:, lo:lo + half])
            tops.append(_topk_rows(scores, None, k)[:2])
        (s0, i0), (s1, i1) = tops
        cand_s = jnp.concatenate([s0[a:a + 1, :] + s1 for a in range(k)], axis=0)
        cand_i = jnp.concatenate(
            [i0[a:a + 1, :] * float(N_KEYS) + i1 for a in range(k)], axis=0)
        best_s, _, best_i = _topk_rows(cand_s, cand_i, k)
        e = jnp.exp(best_s - best_s[0:1, :])
        gate_ref[hd * k:(hd + 1) * k, :] = e / jnp.sum(e, axis=0, keepdims=True)
        idx_ref[hd * k:(hd + 1) * k, :] = best_i.astype(jnp.int32)


def _route(hn, wq_bf, keys_bf, heads, tm=256):
    t, d = hn.shape
    nq = wq_bf.shape[1]
    slots = heads * PEER_TOPK
    kern = functools.partial(_route_kernel, heads=heads)
    return pl.pallas_call(
        kern,
        out_shape=(jax.ShapeDtypeStruct((slots, t), jnp.int32),
                   jax.ShapeDtypeStruct((slots, t), F32)),
        grid=(t // tm,),
        in_specs=[pl.BlockSpec((tm, d), lambda i: (i, 0)),
                  pl.BlockSpec((d, nq), lambda i: (0, 0)),
                  pl.BlockSpec(keys_bf.shape, lambda i: (0, 0, 0, 0))],
        out_specs=(pl.BlockSpec((slots, tm), lambda i: (0, i)),
                   pl.BlockSpec((slots, tm), lambda i: (0, i))),
        compiler_params=_cparams(1),
        name="route",
    )(hn, wq_bf, keys_bf)


def _pack_table(w):
    e, d = w.shape
    bits = lax.bitcast_convert_type(w.astype(BF16), jnp.uint16).astype(jnp.uint32)
    lo, hi = bits[:, :d // 2], bits[:, d // 2:]
    return (lo | (hi << 16)).reshape(e, d // 256, 128)


def _gather_rows(idx_ref, tab_ref, tile_ref, base, slots, chunks):
    for k in range(slots):
        slab = tab_ref[idx_ref[base + k]]
        tile_ref[pl.ds(k, chunks, stride=TILE_STRIDE), :] = slab


def _tile_chunk(tile_ref, c, slots):
    return pltpu.bitcast(tile_ref[c * TILE_STRIDE:c * TILE_STRIDE + slots, :], BF16)


def _peer_u_kernel(idx_ref, hn_ref, gate_ref, tab_ref, w_ref, tile_ref, z_ref):
    tm = GATHER_TOKENS
    slots = gate_ref.shape[1]
    chunks = tab_ref.shape[1]
    half = chunks * 128
    rhs = [jnp.concatenate([hn_ref[:, c * 128:(c + 1) * 128],
                            hn_ref[:, half + c * 128:half + (c + 1) * 128]], axis=0)
           for c in range(chunks)]
    z_ref[...] = jnp.zeros_like(z_ref)
    lane = lax.broadcasted_iota(jnp.int32, (2 * slots, 2 * tm), 1)

    def token(t, carry):
        _gather_rows(idx_ref, tab_ref, tile_ref, t * slots, slots, chunks)
        y = _dot_nt(_tile_chunk(tile_ref, 0, slots), rhs[0])
        for c in range(1, chunks):
            y = y + _dot_nt(_tile_chunk(tile_ref, c, slots), rhs[c])
        z_ref[...] += jnp.where((lane == t) | (lane == t + tm), y, 0.0)
        return carry

    lax.fori_loop(0, tm, token, 0)
    act_t = (z_ref[pl.ds(0, slots, stride=2), :][:, 0:tm]
             + z_ref[pl.ds(1, slots, stride=2), :][:, tm:2 * tm])
    pre = act_t.T
    act = 0.5 * pre * (1.0 + lax.erf(pre * math.sqrt(0.5)))
    w_ref[...] = gate_ref[...] * act


def _peer_u(idx_flat, hn, gates, table, tm=GATHER_TOKENS):
    t, d = hn.shape
    slots = gates.shape[1]
    return pl.pallas_call(
        _peer_u_kernel,
        out_shape=jax.ShapeDtypeStruct((t, slots), F32),
        grid=(t // tm,),
        in_specs=[pl.BlockSpec((tm * slots,), lambda i: (i,), memory_space=pltpu.SMEM),
                  pl.BlockSpec((tm, d), lambda i: (i, 0)),
                  pl.BlockSpec((tm, slots), lambda i: (i, 0)),
                  pl.BlockSpec(table.shape, lambda i: (0, 0, 0),
                               pipeline_mode=pl.Buffered(1))],
        out_specs=pl.BlockSpec((tm, slots), lambda i: (i, 0)),
        scratch_shapes=[pltpu.VMEM((table.shape[1] * TILE_STRIDE, 128), jnp.uint32),
                        pltpu.VMEM((2 * slots, 2 * tm), F32)],
        compiler_params=_cparams(1),
        name="peer_u",
    )(idx_flat, hn, gates, table)


def _peer_v_kernel(idx_ref, w_ref, x1_ref, tab_ref, x2_ref, tile_ref, lhs_ref):
    tm = GATHER_TOKENS
    slots = w_ref.shape[1]
    chunks = tab_ref.shape[1]
    half = chunks * 128
    r = lax.broadcasted_iota(jnp.int32, (slots, 2 * slots), 0)
    cc = lax.broadcasted_iota(jnp.int32, (slots, 2 * slots), 1)
    even = jnp.where(cc == 2 * r, 1.0, 0.0).astype(BF16)
    odd = jnp.where(cc == 2 * r + 1, 1.0, 0.0).astype(BF16)
    w_hi, w_lo = _split_bf16(w_ref[...])
    lhs_ref[0] = _dot(w_hi, even)
    lhs_ref[1] = _dot(w_hi, odd)
    lhs_ref[2] = _dot(w_lo, even)
    lhs_ref[3] = _dot(w_lo, odd)
    x2_ref[...] = x1_ref[...]
    sub = lax.broadcasted_iota(jnp.int32, (8, 128), 0)

    def token(t, carry):
        _gather_rows(idx_ref, tab_ref, tile_ref, t * slots, slots, chunks)
        g8 = pl.multiple_of((t >> 3) << 3, 8)
        lhs = jnp.concatenate([lhs_ref[v, pl.ds(g8, 8), :] for v in range(4)],
                              axis=0).astype(BF16)
        mine = sub == (t & 7)
        for c in range(chunks):
            res = _dot(lhs, _tile_chunk(tile_ref, c, slots))
            lo_feat = jnp.where(mine, res[0:8] + res[16:24], 0.0)
            hi_feat = jnp.where(mine, res[8:16] + res[24:32], 0.0)
            x2_ref[pl.ds(g8, 8), c * 128:(c + 1) * 128] += lo_feat
            x2_ref[pl.ds(g8, 8), half + c * 128:half + (c + 1) * 128] += hi_feat
        return carry

    lax.fori_loop(0, tm, token, 0)


def _peer_v(idx_flat, w, x1, table, tm=GATHER_TOKENS):
    t, d = x1.shape
    slots = w.shape[1]
    return pl.pallas_call(
        _peer_v_kernel,
        out_shape=jax.ShapeDtypeStruct((t, d), F32),
        grid=(t // tm,),
        in_specs=[pl.BlockSpec((tm * slots,), lambda i: (i,), memory_space=pltpu.SMEM),
                  pl.BlockSpec((tm, slots), lambda i: (i, 0)),
                  pl.BlockSpec((tm, d), lambda i: (i, 0)),
                  pl.BlockSpec(table.shape, lambda i: (0, 0, 0),
                               pipeline_mode=pl.Buffered(1))],
        out_specs=pl.BlockSpec((tm, d), lambda i: (i, 0)),
        scratch_shapes=[pltpu.VMEM((table.shape[1] * TILE_STRIDE, 128), jnp.uint32),
                        pltpu.VMEM((4, tm, 2 * slots), F32)],
        compiler_params=_cparams(1),
        name="peer_v",
    )(idx_flat, w, x1, table)


def _ple_kernel(x_ref, p_ref, gp_ref, wg_ref, wp_ref, gf_ref, o_ref, *, final_norm):
    x = x_ref[...]
    gate = jax.nn.sigmoid(_dot((_rms(x) * gp_ref[...]).astype(BF16), wg_ref[...]))
    x = x + gate * _dot(p_ref[...].astype(BF16), wp_ref[...])
    o_ref[...] = _rms(x) * gf_ref[...] if final_norm else x


def _ple(x2, p2, gp, wg, wp, gf, final_norm, tm=512):
    t, d = x2.shape
    pd = p2.shape[1]
    return pl.pallas_call(
        functools.partial(_ple_kernel, final_norm=final_norm),
        out_shape=jax.ShapeDtypeStruct((t, d), F32),
        grid=(t // tm,),
        in_specs=[pl.BlockSpec((tm, d), lambda i: (i, 0)),
                  pl.BlockSpec((tm, pd), lambda i: (i, 0)),
                  pl.BlockSpec((1, d), lambda i: (0, 0)),
                  pl.BlockSpec((d, d), lambda i: (0, 0)),
                  pl.BlockSpec((pd, d), lambda i: (0, 0)),
                  pl.BlockSpec((1, d), lambda i: (0, 0))],
        out_specs=pl.BlockSpec((tm, d), lambda i: (i, 0)),
        compiler_params=_cparams(1),
        name="ple",
    )(x2, p2, gp.reshape(1, d), wg, wp, gf.reshape(1, d))


def kernel(x, p, norm_mix, w_in, w_att_up, w_pool_group, pool_scale, w_out, norm_ffn,
           w_query, sub_keys, expert_u, expert_v, norm_ple, w_ple_gate, w_ple, norm_final):
    batch, seq, d = x.shape
    depth = w_in.shape[0]
    heads, head_dim = w_att_up.shape[1], w_att_up.shape[2]
    pool = w_pool_group.shape[1] * w_pool_group.shape[2]
    peer_heads = sub_keys.shape[1]
    xt = x.reshape(batch * seq, d)
    for i in range(depth):
        q, k, v, u, ga, gp = _in_proj(xt, norm_mix[i], w_in[i].astype(BF16),
                                      heads, head_dim, pool)
        att = _attention(q, k, v, batch, seq)
        x1, hn = _mix(xt, att, u, ga, gp, w_att_up[i].astype(BF16),
                      w_pool_group[i].astype(BF16), pool_scale[i], w_out[i].astype(BF16),
                      norm_ffn[i], batch, seq)
        idx_t, gate_t = _route(hn, w_query[i].astype(BF16), sub_keys[i].astype(BF16),
                               peer_heads)
        idx_flat = idx_t.T.reshape(-1)
        gates = gate_t.T
        w = _peer_u(idx_flat, hn, gates, _pack_table(expert_u[i]))
        x2 = _peer_v(idx_flat, w, x1, _pack_table(expert_v[i]))
        xt = _ple(x2, p[i].reshape(batch * seq, -1), norm_ple[i],
                  w_ple_gate[i].astype(BF16), w_ple[i].astype(BF16), norm_final,
                  final_norm=(i == depth - 1))
    return xt.reshape(batch, seq, d)
```

```python
import functools
import math

import jax
import jax.numpy as jnp
from jax import lax
from jax.experimental import pallas as pl
from jax.experimental.pallas import tpu as pltpu

F32 = jnp.float32
BF16 = jnp.bfloat16

EPS = 1e-6
ATT_BLOCK = 128
POOL_WINDOWS = (2, 4, 8, 16)
PEER_TOPK = 16
N_KEYS = 128
V7X_VMEM_BYTES = 64 * 1024 * 1024
VMEM_LIMIT = V7X_VMEM_BYTES - 8 * 1024 * 1024
GATHER_TOKENS = 64
TILE_STRIDE = 136
NEG_INF = float("-inf")


def _cparams(n_axes):
    return pltpu.CompilerParams(
        dimension_semantics=("arbitrary",) * n_axes, vmem_limit_bytes=VMEM_LIMIT)


def _rms(x):
    return x * lax.rsqrt(jnp.mean(x * x, axis=-1, keepdims=True) + EPS)


def _dot(a, b):
    return jnp.dot(a, b, preferred_element_type=F32)


def _dot_nt(a, b):
    return lax.dot_general(a, b, (((1,), (1,)), ((), ())), preferred_element_type=F32)


def _split_bf16(x):
    hi = x.astype(BF16)
    lo = (x - hi.astype(F32)).astype(BF16)
    return hi, lo


def _in_proj_kernel(x_ref, g_ref, w_ref, q_ref, k_ref, v_ref, u_ref, ga_ref, gp_ref,
                    *, head_dim):
    h = (_rms(x_ref[...]) * g_ref[...]).astype(BF16)
    scale = head_dim ** -0.5
    lo = 0
    for ref, mul in ((q_ref, scale), (k_ref, None), (v_ref, None), (u_ref, None),
                     (ga_ref, None), (gp_ref, None)):
        width = ref.shape[1]
        y = _dot(h, w_ref[:, lo:lo + width])
        ref[...] = (y if mul is None else y * mul).astype(BF16)
        lo += width


def _in_proj(x2, g, w_bf, att, head_dim, pool, tm=512):
    t, d = x2.shape
    n = w_bf.shape[1]
    kern = functools.partial(_in_proj_kernel, head_dim=head_dim)
    widths = (att, att, att, pool, d, d)
    return pl.pallas_call(
        kern,
        out_shape=tuple(jax.ShapeDtypeStruct((t, w), BF16) for w in widths),
        grid=(t // tm,),
        in_specs=[pl.BlockSpec((tm, d), lambda i: (i, 0)),
                  pl.BlockSpec((1, d), lambda i: (0, 0)),
                  pl.BlockSpec((d, n), lambda i: (0, 0))],
        out_specs=tuple(pl.BlockSpec((tm, w), lambda i: (i, 0)) for w in widths),
        compiler_params=_cparams(1),
        name="in_proj",
    )(x2, g.reshape(1, d), w_bf)


def _attn_kernel(q_ref, k_ref, v_ref, o_ref, qm_ref, tri_ref, acc_ref, surv_ref,
                 *, head_dim):
    blk = ATT_BLOCK
    i = pl.program_id(1)
    pairs = q_ref.shape[1] // blk
    lane = lax.broadcasted_iota(jnp.int32, (blk, blk), 1)
    sub = lax.broadcasted_iota(jnp.int32, (blk, blk), 0)
    first = lane < head_dim
    for p in range(pairs):
        qp = q_ref[:, p * blk:(p + 1) * blk]
        zero = jnp.zeros_like(qp)
        qm_ref[p, 0:blk, :] = jnp.where(first, qp, zero)
        qm_ref[p, blk:2 * blk, :] = jnp.where(first, zero, qp)
    r2 = lax.broadcasted_iota(jnp.int32, (2 * blk, 2 * blk), 0) & (blk - 1)
    c2 = lax.broadcasted_iota(jnp.int32, (2 * blk, 2 * blk), 1)
    tri_ref[...] = jnp.where((r2 > c2) | (c2 >= blk), 1.0, 0.0).astype(BF16)
    acc_ref[...] = jnp.zeros_like(acc_ref)
    surv_ref[...] = jnp.zeros_like(surv_ref)
    causal = jnp.concatenate([lane < sub, lane < sub], axis=0)

    def block(j, mask):
        start = pl.multiple_of(j * blk, blk)
        log_beta, parts = [], []
        for p in range(pairs):
            kp = k_ref[pl.ds(start, blk), p * blk:(p + 1) * blk]
            z = _dot_nt(qm_ref[p], kp)
            sp = jnp.maximum(z, 0.0) + jnp.log(1.0 + jnp.exp(-jnp.abs(z)))
            log_beta.append(z - sp)
            if mask is not None:
                sp = jnp.where(mask, sp, 0.0)
            hi, lo = _split_bf16(sp)
            parts.append(jnp.concatenate([hi, lo], axis=1))
        sums = _dot(jnp.concatenate(parts, axis=0), tri_ref[...])
        for p in range(pairs):
            rows = slice(p * 2 * blk, (p + 1) * 2 * blk)
            later, total = sums[rows, :blk], sums[rows, blk:]
            w = jnp.exp(log_beta[p] - later - surv_ref[p])
            if mask is not None:
                w = jnp.where(mask, w, 0.0)
            vp = v_ref[pl.ds(start, blk), p * blk:(p + 1) * blk]
            acc_ref[p] += _dot(w.astype(BF16), vp)
            surv_ref[p] += total

    block(i, causal)

    def body(n, carry):
        block(i - 1 - 2 * n, None)
        block(i - 2 - 2 * n, None)
        return carry

    lax.fori_loop(0, i >> 1, body, 0)

    @pl.when((i & 1) == 1)
    def _():
        block(0, None)
    for p in range(pairs):
        o_ref[:, p * blk:(p + 1) * blk] = jnp.where(
            first, acc_ref[p, 0:blk, :], acc_ref[p, blk:2 * blk, :]).astype(o_ref.dtype)


def _attention(q, k, v, batch, seq, head_dim):
    t, width = q.shape
    blk = ATT_BLOCK
    assert 2 * head_dim == blk and width % blk == 0
    pairs = width // blk
    nq = seq // blk
    qspec = pl.BlockSpec((blk, width), lambda b, i: (b * nq + i, 0))
    kvspec = pl.BlockSpec((seq, width), lambda b, i: (b, 0))
    return pl.pallas_call(
        functools.partial(_attn_kernel, head_dim=head_dim),
        out_shape=jax.ShapeDtypeStruct((t, width), BF16),
        grid=(batch, nq),
        in_specs=[qspec, kvspec, kvspec],
        out_specs=qspec,
        scratch_shapes=[pltpu.VMEM((pairs, 2 * blk, blk), BF16),
                        pltpu.VMEM((2 * blk, 2 * blk), BF16),
                        pltpu.VMEM((pairs, 2 * blk, blk), F32),
                        pltpu.VMEM((pairs, 2 * blk, blk), F32)],
        compiler_params=_cparams(2),
        name="attn",
    )(q, k, v)


def _mix_kernel(x_ref, att_ref, u_ref, ga_ref, gp_ref, wa_ref, wp_ref, ps_ref, wo_ref,
                gn_ref, x1_ref, hn_ref, prev_ref, *, tm):
    s = pl.program_id(1)

    @pl.when(s == 0)
    def _():
        prev_ref[...] = jnp.zeros_like(prev_ref)

    pairs, pw, _ = wa_ref.shape
    a_up = jnp.concatenate(
        [_dot(att_ref[:, p * pw:(p + 1) * pw], wa_ref[p]) for p in range(pairs)],
        axis=1)

    u = u_ref[...]
    ext = jnp.concatenate([prev_ref[...], u], axis=0)
    prev_ref[...] = u
    r = lax.broadcasted_iota(jnp.int32, (tm, 2 * tm), 0) + tm
    c = lax.broadcasted_iota(jnp.int32, (tm, 2 * tm), 1)
    pos = (s * tm + lax.broadcasted_iota(jnp.int32, (tm, 1), 0) + 1).astype(F32)
    groups = len(POOL_WINDOWS)
    gdim = u.shape[1] // groups
    uf = u.astype(F32)
    p_parts = []
    for g, w in enumerate(POOL_WINDOWS):
        band = jnp.where((c <= r) & (c > r - w), 1.0, 0.0).astype(BF16)
        sl = slice(g * gdim, (g + 1) * gdim)
        window_sum = _dot(band, ext[:, sl])
        y = window_sum / jnp.minimum(pos, float(w)) - uf[:, sl]
        p_parts.append(_dot(y.astype(BF16), wp_ref[g]))
    p_up = jnp.concatenate(p_parts, axis=1) * ps_ref[...]

    merged = (jax.nn.sigmoid(ga_ref[...].astype(F32)) * a_up
              + jax.nn.sigmoid(gp_ref[...].astype(F32)) * p_up)
    x1 = x_ref[...] + _dot(merged.astype(BF16), wo_ref[...])
    x1_ref[...] = x1
    hn_ref[...] = (_rms(x1) * gn_ref[...]).astype(BF16)


def _pair_block_diag(w):
    h, a, b = w.shape
    z = jnp.zeros((h // 2, a, b), w.dtype)
    top = jnp.concatenate([w[0::2], z], axis=2)
    bottom = jnp.concatenate([z, w[1::2]], axis=2)
    return jnp.concatenate([top, bottom], axis=1)


def _mix(x2, att, u, ga, gp, wa, wp, ps, wo, gn, batch, seq, tm=256):
    t, d = x2.shape
    att_w = att.shape[1]
    pool = u.shape[1]
    ns = seq // tm
    tok = lambda b, s: (b * ns + s, 0)
    const2 = lambda b, s: (0, 0)
    const3 = lambda b, s: (0, 0, 0)
    kern = functools.partial(_mix_kernel, tm=tm)
    return pl.pallas_call(
        kern,
        out_shape=(jax.ShapeDtypeStruct((t, d), F32), jax.ShapeDtypeStruct((t, d), BF16)),
        grid=(batch, ns),
        in_specs=[pl.BlockSpec((tm, d), tok),
                  pl.BlockSpec((tm, att_w), tok),
                  pl.BlockSpec((tm, pool), tok),
                  pl.BlockSpec((tm, d), tok),
                  pl.BlockSpec((tm, d), tok),
                  pl.BlockSpec(wa.shape, const3),
                  pl.BlockSpec(wp.shape, const3),
                  pl.BlockSpec((1, d), const2),
                  pl.BlockSpec((d, d), const2),
                  pl.BlockSpec((1, d), const2)],
        out_specs=(pl.BlockSpec((tm, d), tok), pl.BlockSpec((tm, d), tok)),
        scratch_shapes=[pltpu.VMEM((tm, pool), BF16)],
        compiler_params=_cparams(2),
        name="mix",
    )(x2, att, u, ga, gp, wa, wp, ps.reshape(1, d), wo, gn.reshape(1, d))


def _topk_rows(s, payload, k):
    rows = s.shape[0]
    pos = lax.broadcasted_iota(jnp.int32, s.shape, 0).astype(F32)
    vals, poss, pays = [], [], []
    for _ in range(k):
        m = jnp.max(s, axis=0, keepdims=True)
        p = jnp.min(jnp.where(s == m, pos, float(rows)), axis=0, keepdims=True)
        hit = pos == p
        if payload is not None:
            pays.append(jnp.sum(jnp.where(hit, payload, 0.0), axis=0, keepdims=True))
        s = jnp.where(hit, NEG_INF, s)
        vals.append(m)
        poss.append(p)
    cat = lambda xs: jnp.concatenate(xs, axis=0)
    return cat(vals), cat(poss), (cat(pays) if payload is not None else None)


def _route_kernel(hn_ref, wq_ref, keys_ref, idx_ref, gate_ref, *, heads):
    q = _dot(hn_ref[...], wq_ref[...]).astype(BF16)
    half = keys_ref.shape[-1]
    k = PEER_TOPK
    for hd in range(heads):
        tops = []
        for c in range(2):
            lo = (hd * 2 + c) * half
            scores = _dot_nt(keys_ref[hd, c], q[:, lo:lo + half])
            tops.append(_topk_rows(scores, None, k)[:2])
        (s0, i0), (s1, i1) = tops
        cand_s = jnp.concatenate([s0[a:a + 1, :] + s1 for a in range(k)], axis=0)
        cand_i = jnp.concatenate(
            [i0[a:a + 1, :] * float(N_KEYS) + i1 for a in range(k)], axis=0)
        best_s, _, best_i = _topk_rows(cand_s, cand_i, k)
        e = jnp.exp(best_s - best_s[0:1, :])
        gate_ref[hd * k:(hd + 1) * k, :] = e / jnp.sum(e, axis=0, keepdims=True)
        idx_ref[hd * k:(hd + 1) * k, :] = best_i.astype(jnp.int32)


def _route(hn, wq_bf, keys_bf, heads, tm=256):
    t, d = hn.shape
    nq = wq_bf.shape[1]
    slots = heads * PEER_TOPK
    kern = functools.partial(_route_kernel, heads=heads)
    return pl.pallas_call(
        kern,
        out_shape=(jax.ShapeDtypeStruct((slots, t), jnp.int32),
                   jax.ShapeDtypeStruct((slots, t), F32)),
        grid=(t // tm,),
        in_specs=[pl.BlockSpec((tm, d), lambda i: (i, 0)),
                  pl.BlockSpec((d, nq), lambda i: (0, 0)),
                  pl.BlockSpec(keys_bf.shape, lambda i: (0, 0, 0, 0))],
        out_specs=(pl.BlockSpec((slots, tm), lambda i: (0, i)),
                   pl.BlockSpec((slots, tm), lambda i: (0, i))),
        compiler_params=_cparams(1),
        name="route",
    )(hn, wq_bf, keys_bf)


def _pack_table(w):
    e, d = w.shape
    bits = lax.bitcast_convert_type(w.astype(BF16), jnp.uint16).astype(jnp.uint32)
    lo, hi = bits[:, :d // 2], bits[:, d // 2:]
    return (lo | (hi << 16)).reshape(e * (d // 256), 128)


def _gather_rows(idx_ref, tab_ref, tile_ref, t, slots, chunks):
    for k in range(slots):
        row = pl.multiple_of(idx_ref[t, k], chunks)
        tile_ref[pl.ds(k, chunks, stride=TILE_STRIDE), :] = tab_ref[pl.ds(row, chunks), :]


def _tile_chunk(tile_ref, c, slots):
    return pltpu.bitcast(tile_ref[c * TILE_STRIDE:c * TILE_STRIDE + slots, :], BF16)


def _for_token_pairs(tm, tiles_ref, gather, compute):
    tile_a, tile_b = tiles_ref.at[0], tiles_ref.at[1]
    gather(0, tile_a)

    def pair(n, carry):
        t0 = 2 * n
        gather(t0 + 1, tile_b)
        compute(t0, tile_a)
        gather(jnp.minimum(t0 + 2, tm - 1), tile_a)
        compute(t0 + 1, tile_b)
        return carry

    lax.fori_loop(0, tm // 2, pair, 0)


def _gather_specs(tm, slots, table):
    return [pl.BlockSpec((tm, slots), lambda i: (i, 0), memory_space=pltpu.SMEM),
            pl.BlockSpec(table.shape, lambda i: (0, 0), pipeline_mode=pl.Buffered(1))]


def _peer_u_kernel(idx_ref, hn_ref, gate_ref, tab_ref, w_ref, tiles_ref, z_ref, *, chunks):
    tm = GATHER_TOKENS
    slots = gate_ref.shape[1]
    half = chunks * 128
    rhs = [jnp.concatenate([hn_ref[:, c * 128:(c + 1) * 128],
                            hn_ref[:, half + c * 128:half + (c + 1) * 128]], axis=0)
           for c in range(chunks)]
    z_ref[...] = jnp.zeros_like(z_ref)
    lane = lax.broadcasted_iota(jnp.int32, (2 * slots, 2 * tm), 1)

    def gather(t, tile):
        _gather_rows(idx_ref, tab_ref, tile, t, slots, chunks)

    def compute(t, tile):
        y = _dot_nt(_tile_chunk(tile, 0, slots), rhs[0])
        for c in range(1, chunks):
            y = y + _dot_nt(_tile_chunk(tile, c, slots), rhs[c])
        z_ref[...] += jnp.where((lane == t) | (lane == t + tm), y, 0.0)

    _for_token_pairs(tm, tiles_ref, gather, compute)
    act_t = (z_ref[pl.ds(0, slots, stride=2), :][:, 0:tm]
             + z_ref[pl.ds(1, slots, stride=2), :][:, tm:2 * tm])
    pre = act_t.T
    act = 0.5 * pre * (1.0 + lax.erf(pre * math.sqrt(0.5)))
    w_ref[...] = gate_ref[...] * act


def _peer_u(idx_rows, hn, gates, table, chunks, tm=GATHER_TOKENS):
    t, d = hn.shape
    slots = gates.shape[1]
    idx_spec, tab_spec = _gather_specs(tm, slots, table)
    return pl.pallas_call(
        functools.partial(_peer_u_kernel, chunks=chunks),
        out_shape=jax.ShapeDtypeStruct((t, slots), F32),
        grid=(t // tm,),
        in_specs=[idx_spec,
                  pl.BlockSpec((tm, d), lambda i: (i, 0)),
                  pl.BlockSpec((tm, slots), lambda i: (i, 0)),
                  tab_spec],
        out_specs=pl.BlockSpec((tm, slots), lambda i: (i, 0)),
        scratch_shapes=[pltpu.VMEM((2, chunks * TILE_STRIDE, 128), jnp.uint32),
                        pltpu.VMEM((2 * slots, 2 * tm), F32)],
        compiler_params=_cparams(1),
        name="peer_u",
    )(idx_rows, hn, gates, table)


def _peer_v_kernel(idx_ref, w_ref, x1_ref, tab_ref, x2_ref, tiles_ref, lhs_ref, *, chunks):
    tm = GATHER_TOKENS
    slots = w_ref.shape[1]
    half = chunks * 128
    r = lax.broadcasted_iota(jnp.int32, (slots, 2 * slots), 0)
    cc = lax.broadcasted_iota(jnp.int32, (slots, 2 * slots), 1)
    even = jnp.where(cc == 2 * r, 1.0, 0.0).astype(BF16)
    odd = jnp.where(cc == 2 * r + 1, 1.0, 0.0).astype(BF16)
    w_hi, w_lo = _split_bf16(w_ref[...])
    lhs_ref[0] = _dot(w_hi, even)
    lhs_ref[1] = _dot(w_hi, odd)
    lhs_ref[2] = _dot(w_lo, even)
    lhs_ref[3] = _dot(w_lo, odd)
    x2_ref[...] = x1_ref[...]
    sub = lax.broadcasted_iota(jnp.int32, (8, 128), 0)

    def gather(t, tile):
        _gather_rows(idx_ref, tab_ref, tile, t, slots, chunks)

    def compute(t, tile):
        g8 = pl.multiple_of((t >> 3) << 3, 8)
        lhs = jnp.concatenate([lhs_ref[v, pl.ds(g8, 8), :] for v in range(4)],
                              axis=0).astype(BF16)
        mine = sub == (t & 7)
        for c in range(chunks):
            res = _dot(lhs, _tile_chunk(tile, c, slots))
            lo_feat = jnp.where(mine, res[0:8] + res[16:24], 0.0)
            hi_feat = jnp.where(mine, res[8:16] + res[24:32], 0.0)
            x2_ref[pl.ds(g8, 8), c * 128:(c + 1) * 128] += lo_feat
            x2_ref[pl.ds(g8, 8), half + c * 128:half + (c + 1) * 128] += hi_feat

    _for_token_pairs(tm, tiles_ref, gather, compute)


def _peer_v(idx_rows, w, x1, table, chunks, tm=GATHER_TOKENS):
    t, d = x1.shape
    slots = w.shape[1]
    idx_spec, tab_spec = _gather_specs(tm, slots, table)
    return pl.pallas_call(
        functools.partial(_peer_v_kernel, chunks=chunks),
        out_shape=jax.ShapeDtypeStruct((t, d), F32),
        grid=(t // tm,),
        in_specs=[idx_spec,
                  pl.BlockSpec((tm, slots), lambda i: (i, 0)),
                  pl.BlockSpec((tm, d), lambda i: (i, 0)),
                  tab_spec],
        out_specs=pl.BlockSpec((tm, d), lambda i: (i, 0)),
        scratch_shapes=[pltpu.VMEM((2, chunks * TILE_STRIDE, 128), jnp.uint32),
                        pltpu.VMEM((4, tm, 2 * slots), F32)],
        compiler_params=_cparams(1),
        name="peer_v",
    )(idx_rows, w, x1, table)


def _ple_kernel(x_ref, p_ref, gp_ref, wg_ref, wp_ref, gf_ref, o_ref, *, final_norm):
    x = x_ref[...]
    gate = jax.nn.sigmoid(_dot((_rms(x) * gp_ref[...]).astype(BF16), wg_ref[...]))
    x = x + gate * _dot(p_ref[...].astype(BF16), wp_ref[...])
    o_ref[...] = _rms(x) * gf_ref[...] if final_norm else x


def _ple(x2, p2, gp, wg, wp, gf, final_norm, tm=512):
    t, d = x2.shape
    pd = p2.shape[1]
    return pl.pallas_call(
        functools.partial(_ple_kernel, final_norm=final_norm),
        out_shape=jax.ShapeDtypeStruct((t, d), F32),
        grid=(t // tm,),
        in_specs=[pl.BlockSpec((tm, d), lambda i: (i, 0)),
                  pl.BlockSpec((tm, pd), lambda i: (i, 0)),
                  pl.BlockSpec((1, d), lambda i: (0, 0)),
                  pl.BlockSpec((d, d), lambda i: (0, 0)),
                  pl.BlockSpec((pd, d), lambda i: (0, 0)),
                  pl.BlockSpec((1, d), lambda i: (0, 0))],
        out_specs=pl.BlockSpec((tm, d), lambda i: (i, 0)),
        compiler_params=_cparams(1),
        name="ple",
    )(x2, p2, gp.reshape(1, d), wg, wp, gf.reshape(1, d))


def kernel(x, p, norm_mix, w_in, w_att_up, w_pool_group, pool_scale, w_out, norm_ffn,
           w_query, sub_keys, expert_u, expert_v, norm_ple, w_ple_gate, w_ple, norm_final):
    batch, seq, d = x.shape
    depth = w_in.shape[0]
    heads, head_dim = w_att_up.shape[1], w_att_up.shape[2]
    pool = w_pool_group.shape[1] * w_pool_group.shape[2]
    peer_heads = sub_keys.shape[1]
    xt = x.reshape(batch * seq, d)
    for i in range(depth):
        q, k, v, u, ga, gp = _in_proj(xt, norm_mix[i], w_in[i].astype(BF16),
                                      heads * head_dim, head_dim, pool)
        att = _attention(q, k, v, batch, seq, head_dim)
        x1, hn = _mix(xt, att, u, ga, gp, _pair_block_diag(w_att_up[i]).astype(BF16),
                      w_pool_group[i].astype(BF16), pool_scale[i], w_out[i].astype(BF16),
                      norm_ffn[i], batch, seq)
        idx_t, gate_t = _route(hn, w_query[i].astype(BF16), sub_keys[i].astype(BF16),
                               peer_heads)
        chunks = d // 256
        idx_rows = idx_t.T * chunks
        gates = gate_t.T
        w = _peer_u(idx_rows, hn, gates, _pack_table(expert_u[i]), chunks)
        x2 = _peer_v(idx_rows, w, x1, _pack_table(expert_v[i]), chunks)
        xt = _ple(x2, p[i].reshape(batch * seq, -1), norm_ple[i],
                  w_ple_gate[i].astype(BF16), w_ple[i].astype(BF16), norm_final,
                  final_norm=(i == depth - 1))
    return xt.reshape(batch, seq, d)
```

```python
import functools
import math

import jax
import jax.numpy as jnp
from jax import lax
from jax.experimental import pallas as pl
from jax.experimental.pallas import tpu as pltpu

F32 = jnp.float32
BF16 = jnp.bfloat16

EPS = 1e-6
ATT_BLOCK = 128
SURVIVAL_UNDERFLOW = 150.0
POOL_WINDOWS = (2, 4, 8, 16)
PEER_TOPK = 16
N_KEYS = 128
V7X_VMEM_BYTES = 64 * 1024 * 1024
VMEM_LIMIT = V7X_VMEM_BYTES - 8 * 1024 * 1024
GATHER_TOKENS = 64
TILE_STRIDE = 136
NEG_INF = float("-inf")
POS_SENTINEL = 1e9


def _cparams(n_axes):
    return pltpu.CompilerParams(
        dimension_semantics=("arbitrary",) * n_axes, vmem_limit_bytes=VMEM_LIMIT)


def _rms(x):
    return x * lax.rsqrt(jnp.mean(x * x, axis=-1, keepdims=True) + EPS)


def _dot(a, b):
    return jnp.dot(a, b, preferred_element_type=F32)


def _dot_nt(a, b):
    return lax.dot_general(a, b, (((1,), (1,)), ((), ())), preferred_element_type=F32)


def _split_bf16(x):
    hi = x.astype(BF16)
    lo = (x - hi.astype(F32)).astype(BF16)
    return hi, lo


def _in_proj_kernel(x_ref, g_ref, w_ref, q_ref, k_ref, v_ref, u_ref, ga_ref, gp_ref,
                    *, head_dim):
    h = (_rms(x_ref[...]) * g_ref[...]).astype(BF16)
    scale = head_dim ** -0.5
    lo = 0
    for ref, mul in ((q_ref, scale), (k_ref, None), (v_ref, None), (u_ref, None),
                     (ga_ref, None), (gp_ref, None)):
        width = ref.shape[1]
        y = _dot(h, w_ref[:, lo:lo + width])
        ref[...] = (y if mul is None else y * mul).astype(BF16)
        lo += width


def _in_proj(x2, g, w_bf, att, head_dim, pool, tm=512):
    t, d = x2.shape
    n = w_bf.shape[1]
    kern = functools.partial(_in_proj_kernel, head_dim=head_dim)
    widths = (att, att, att, pool, d, d)
    return pl.pallas_call(
        kern,
        out_shape=tuple(jax.ShapeDtypeStruct((t, w), BF16) for w in widths),
        grid=(t // tm,),
        in_specs=[pl.BlockSpec((tm, d), lambda i: (i, 0)),
                  pl.BlockSpec((1, d), lambda i: (0, 0)),
                  pl.BlockSpec((d, n), lambda i: (0, 0))],
        out_specs=tuple(pl.BlockSpec((tm, w), lambda i: (i, 0)) for w in widths),
        compiler_params=_cparams(1),
        name="in_proj",
    )(x2, g.reshape(1, d), w_bf)


def _attn_kernel(q_ref, k_ref, v_ref, o_ref, qm_ref, tri_ref, acc_ref, surv_ref,
                 *, head_dim):
    blk = ATT_BLOCK
    i = pl.program_id(1)
    pairs = q_ref.shape[1] // blk
    lane = lax.broadcasted_iota(jnp.int32, (blk, blk), 1)
    sub = lax.broadcasted_iota(jnp.int32, (blk, blk), 0)
    first = lane < head_dim
    for p in range(pairs):
        qp = q_ref[:, p * blk:(p + 1) * blk]
        zero = jnp.zeros_like(qp)
        qm_ref[p, 0:blk, :] = jnp.where(first, qp, zero)
        qm_ref[p, blk:2 * blk, :] = jnp.where(first, zero, qp)
    r2 = lax.broadcasted_iota(jnp.int32, (2 * blk, 2 * blk), 0) & (blk - 1)
    c2 = lax.broadcasted_iota(jnp.int32, (2 * blk, 2 * blk), 1)
    tri_ref[...] = jnp.where((r2 > c2) | (c2 >= blk), 1.0, 0.0).astype(BF16)
    acc_ref[...] = jnp.zeros_like(acc_ref)
    surv_ref[...] = jnp.zeros_like(surv_ref)
    causal = jnp.concatenate([lane < sub, lane < sub], axis=0)

    def block(j, mask):
        start = pl.multiple_of(j * blk, blk)
        log_beta, parts = [], []
        for p in range(pairs):
            kp = k_ref[pl.ds(start, blk), p * blk:(p + 1) * blk]
            z = _dot_nt(qm_ref[p], kp)
            sp = jnp.maximum(z, 0.0) + jnp.log(1.0 + jnp.exp(-jnp.abs(z)))
            log_beta.append(z - sp)
            if mask is not None:
                sp = jnp.where(mask, sp, 0.0)
            hi, lo = _split_bf16(sp)
            parts.append(jnp.concatenate([hi, lo], axis=1))
        sums = _dot(jnp.concatenate(parts, axis=0), tri_ref[...])
        for p in range(pairs):
            rows = slice(p * 2 * blk, (p + 1) * 2 * blk)
            later, total = sums[rows, :blk], sums[rows, blk:]
            w = jnp.exp(log_beta[p] - later - surv_ref[p])
            if mask is not None:
                w = jnp.where(mask, w, 0.0)
            vp = v_ref[pl.ds(start, blk), p * blk:(p + 1) * blk]
            acc_ref[p] += _dot(w.astype(BF16), vp)
            surv_ref[p] += total

    def any_alive():
        return jnp.min(surv_ref[...]) < SURVIVAL_UNDERFLOW

    block(i, causal)

    def cond(carry):
        return (carry[0] < (i >> 1)) & carry[1]

    def body(carry):
        n = carry[0]
        block(i - 1 - 2 * n, None)
        block(i - 2 - 2 * n, None)
        return n + 1, any_alive()

    _, alive = lax.while_loop(cond, body, (jnp.int32(0), any_alive()))

    @pl.when(((i & 1) == 1) & alive)
    def _():
        block(0, None)

    for p in range(pairs):
        o_ref[:, p * blk:(p + 1) * blk] = jnp.where(
            first, acc_ref[p, 0:blk, :], acc_ref[p, blk:2 * blk, :]).astype(o_ref.dtype)


def _attention(q, k, v, batch, seq, head_dim):
    t, width = q.shape
    blk = ATT_BLOCK
    assert 2 * head_dim == blk and width % blk == 0
    pairs = width // blk
    nq = seq // blk
    qspec = pl.BlockSpec((blk, width), lambda b, i: (b * nq + i, 0))
    kvspec = pl.BlockSpec((seq, width), lambda b, i: (b, 0))
    return pl.pallas_call(
        functools.partial(_attn_kernel, head_dim=head_dim),
        out_shape=jax.ShapeDtypeStruct((t, width), BF16),
        grid=(batch, nq),
        in_specs=[qspec, kvspec, kvspec],
        out_specs=qspec,
        scratch_shapes=[pltpu.VMEM((pairs, 2 * blk, blk), BF16),
                        pltpu.VMEM((2 * blk, 2 * blk), BF16),
                        pltpu.VMEM((pairs, 2 * blk, blk), F32),
                        pltpu.VMEM((pairs, 2 * blk, blk), F32)],
        compiler_params=_cparams(2),
        name="attn",
    )(q, k, v)


def _mix_kernel(x_ref, att_ref, u_ref, ga_ref, gp_ref, wa_ref, wp_ref, ps_ref, wo_ref,
                gn_ref, x1_ref, hn_ref, prev_ref, *, tm):
    s = pl.program_id(1)

    @pl.when(s == 0)
    def _():
        prev_ref[...] = jnp.zeros_like(prev_ref)

    pairs, pw, _ = wa_ref.shape
    a_up = jnp.concatenate(
        [_dot(att_ref[:, p * pw:(p + 1) * pw], wa_ref[p]) for p in range(pairs)],
        axis=1)

    u = u_ref[...]
    ext = jnp.concatenate([prev_ref[...], u], axis=0)
    prev_ref[...] = u
    r = lax.broadcasted_iota(jnp.int32, (tm, 2 * tm), 0) + tm
    c = lax.broadcasted_iota(jnp.int32, (tm, 2 * tm), 1)
    pos = (s * tm + lax.broadcasted_iota(jnp.int32, (tm, 1), 0) + 1).astype(F32)
    groups = len(POOL_WINDOWS)
    gdim = u.shape[1] // groups
    uf = u.astype(F32)
    p_parts = []
    for g, w in enumerate(POOL_WINDOWS):
        band = jnp.where((c <= r) & (c > r - w), 1.0, 0.0).astype(BF16)
        sl = slice(g * gdim, (g + 1) * gdim)
        window_sum = _dot(band, ext[:, sl])
        y = window_sum / jnp.minimum(pos, float(w)) - uf[:, sl]
        p_parts.append(_dot(y.astype(BF16), wp_ref[g]))
    p_up = jnp.concatenate(p_parts, axis=1) * ps_ref[...]

    merged = (jax.nn.sigmoid(ga_ref[...].astype(F32)) * a_up
              + jax.nn.sigmoid(gp_ref[...].astype(F32)) * p_up)
    x1 = x_ref[...] + _dot(merged.astype(BF16), wo_ref[...])
    x1_ref[...] = x1
    hn_ref[...] = (_rms(x1) * gn_ref[...]).astype(BF16)


def _pair_block_diag(w):
    h, a, b = w.shape
    z = jnp.zeros((h // 2, a, b), w.dtype)
    top = jnp.concatenate([w[0::2], z], axis=2)
    bottom = jnp.concatenate([z, w[1::2]], axis=2)
    return jnp.concatenate([top, bottom], axis=1)


def _mix(x2, att, u, ga, gp, wa, wp, ps, wo, gn, batch, seq, tm=256):
    t, d = x2.shape
    att_w = att.shape[1]
    pool = u.shape[1]
    ns = seq // tm
    tok = lambda b, s: (b * ns + s, 0)
    const2 = lambda b, s: (0, 0)
    const3 = lambda b, s: (0, 0, 0)
    kern = functools.partial(_mix_kernel, tm=tm)
    return pl.pallas_call(
        kern,
        out_shape=(jax.ShapeDtypeStruct((t, d), F32), jax.ShapeDtypeStruct((t, d), BF16)),
        grid=(batch, ns),
        in_specs=[pl.BlockSpec((tm, d), tok),
                  pl.BlockSpec((tm, att_w), tok),
                  pl.BlockSpec((tm, pool), tok),
                  pl.BlockSpec((tm, d), tok),
                  pl.BlockSpec((tm, d), tok),
                  pl.BlockSpec(wa.shape, const3),
                  pl.BlockSpec(wp.shape, const3),
                  pl.BlockSpec((1, d), const2),
                  pl.BlockSpec((d, d), const2),
                  pl.BlockSpec((1, d), const2)],
        out_specs=(pl.BlockSpec((tm, d), tok), pl.BlockSpec((tm, d), tok)),
        scratch_shapes=[pltpu.VMEM((tm, pool), BF16)],
        compiler_params=_cparams(2),
        name="mix",
    )(x2, att, u, ga, gp, wa, wp, ps.reshape(1, d), wo, gn.reshape(1, d))


def _topk_rows(s, payload, k, pos=None):
    if pos is None:
        pos = lax.broadcasted_iota(jnp.int32, s.shape, 0).astype(F32)
    vals, poss, pays = [], [], []
    for _ in range(k):
        m = jnp.max(s, axis=0, keepdims=True)
        p = jnp.min(jnp.where(s == m, pos, POS_SENTINEL), axis=0, keepdims=True)
        hit = pos == p
        if payload is not None:
            pays.append(jnp.sum(jnp.where(hit, payload, 0.0), axis=0, keepdims=True))
        s = jnp.where(hit, NEG_INF, s)
        vals.append(m)
        poss.append(p)
    cat = lambda xs: jnp.concatenate(xs, axis=0)
    return cat(vals), cat(poss), (cat(pays) if payload is not None else None)


def _route_kernel(hn_ref, wq_ref, keys_ref, idx_ref, gate_ref, *, heads):
    q = _dot(hn_ref[...], wq_ref[...]).astype(BF16)
    half = keys_ref.shape[-1]
    k = PEER_TOPK
    for hd in range(heads):
        tops = []
        for c in range(2):
            lo = (hd * 2 + c) * half
            scores = _dot_nt(keys_ref[hd, c], q[:, lo:lo + half])
            tops.append(_topk_rows(scores, None, k)[:2])
        (s0, i0), (s1, i1) = tops
        h = k // 2
        groups = ([(slice(0, 1), slice(0, k))]
                  + [(slice(a, a + 1), slice(0, h)) for a in range(1, h)]
                  + [(slice(h, k), slice(0, 1))])
        cand_s = jnp.concatenate([s0[ra, :] + s1[rb, :] for ra, rb in groups], axis=0)
        cand_i = jnp.concatenate(
            [i0[ra, :] * float(N_KEYS) + i1[rb, :] for ra, rb in groups], axis=0)
        n = s0.shape[1]
        row = lambda lo, cnt: (lax.broadcasted_iota(jnp.int32, (cnt, n), 0) + lo).astype(F32)
        cand_pos = jnp.concatenate(
            [row(0, k)] + [row(a * k, h) for a in range(1, h)] + [row(h, k - h) * float(k)],
            axis=0)
        best_s, _, best_i = _topk_rows(cand_s, cand_i, k, cand_pos)
        e = jnp.exp(best_s - best_s[0:1, :])
        gate_ref[hd * k:(hd + 1) * k, :] = e / jnp.sum(e, axis=0, keepdims=True)
        idx_ref[hd * k:(hd + 1) * k, :] = best_i.astype(jnp.int32)


def _route(hn, wq_bf, keys_bf, heads, tm=256):
    t, d = hn.shape
    nq = wq_bf.shape[1]
    slots = heads * PEER_TOPK
    kern = functools.partial(_route_kernel, heads=heads)
    return pl.pallas_call(
        kern,
        out_shape=(jax.ShapeDtypeStruct((slots, t), jnp.int32),
                   jax.ShapeDtypeStruct((slots, t), F32)),
        grid=(t // tm,),
        in_specs=[pl.BlockSpec((tm, d), lambda i: (i, 0)),
                  pl.BlockSpec((d, nq), lambda i: (0, 0)),
                  pl.BlockSpec(keys_bf.shape, lambda i: (0, 0, 0, 0))],
        out_specs=(pl.BlockSpec((slots, tm), lambda i: (0, i)),
                   pl.BlockSpec((slots, tm), lambda i: (0, i))),
        compiler_params=_cparams(1),
        name="route",
    )(hn, wq_bf, keys_bf)


def _pack_table(w):
    e, d = w.shape
    bits = lax.bitcast_convert_type(w.astype(BF16), jnp.uint16).astype(jnp.uint32)
    lo, hi = bits[:, :d // 2], bits[:, d // 2:]
    return (lo | (hi << 16)).reshape(e * (d // 256), 128)


def _gather_rows(idx_ref, tab_ref, tile_ref, t, slots, chunks):
    for k in range(slots):
        row = pl.multiple_of(idx_ref[t, k], chunks)
        tile_ref[pl.ds(k, chunks, stride=TILE_STRIDE), :] = tab_ref[pl.ds(row, chunks), :]


def _tile_chunk(tile_ref, c, slots):
    return pltpu.bitcast(tile_ref[c * TILE_STRIDE:c * TILE_STRIDE + slots, :], BF16)


def _for_tokens_pipelined(tm, tiles_ref, gather, compute):
    a, b, c, d = (tiles_ref.at[j] for j in range(4))
    gather(0, a)
    gather(1, b)
    last = tm - 1

    def quad(n, carry):
        t0 = 4 * n
        gather(t0 + 2, c)
        gather(t0 + 3, d)
        compute(t0, a)
        compute(t0 + 1, b)
        gather(jnp.minimum(t0 + 4, last), a)
        gather(jnp.minimum(t0 + 5, last), b)
        compute(t0 + 2, c)
        compute(t0 + 3, d)
        return carry

    lax.fori_loop(0, tm // 4, quad, 0)


def _gather_specs(tm, slots, table):
    return [pl.BlockSpec((tm, slots), lambda i: (i, 0), memory_space=pltpu.SMEM),
            pl.BlockSpec(table.shape, lambda i: (0, 0), pipeline_mode=pl.Buffered(1))]


def _peer_u_kernel(idx_ref, hn_ref, gate_ref, tab_ref, w_ref, tiles_ref, z_ref, *, chunks):
    tm = GATHER_TOKENS
    slots = gate_ref.shape[1]
    half = chunks * 128
    hn = hn_ref[...].astype(F32)
    rhs = jnp.concatenate([hn[:, :half], hn[:, half:]], axis=0).T.astype(BF16)
    z_ref[...] = jnp.zeros_like(z_ref)
    lane = lax.broadcasted_iota(jnp.int32, (2 * slots, 2 * tm), 1)

    def gather(t, tile):
        _gather_rows(idx_ref, tab_ref, tile, t, slots, chunks)

    def compute(t, tile):
        rows = jnp.concatenate([_tile_chunk(tile, c, slots) for c in range(chunks)], axis=1)
        y = _dot(rows, rhs)
        z_ref[...] += jnp.where((lane == t) | (lane == t + tm), y, 0.0)

    _for_tokens_pipelined(tm, tiles_ref, gather, compute)
    act_t = (z_ref[pl.ds(0, slots, stride=2), :][:, 0:tm]
             + z_ref[pl.ds(1, slots, stride=2), :][:, tm:2 * tm])
    pre = act_t.T
    act = 0.5 * pre * (1.0 + lax.erf(pre * math.sqrt(0.5)))
    w_ref[...] = gate_ref[...] * act


def _peer_u(idx_rows, hn, gates, table, chunks, tm=GATHER_TOKENS):
    t, d = hn.shape
    slots = gates.shape[1]
    idx_spec, tab_spec = _gather_specs(tm, slots, table)
    return pl.pallas_call(
        functools.partial(_peer_u_kernel, chunks=chunks),
        out_shape=jax.ShapeDtypeStruct((t, slots), F32),
        grid=(t // tm,),
        in_specs=[idx_spec,
                  pl.BlockSpec((tm, d), lambda i: (i, 0)),
                  pl.BlockSpec((tm, slots), lambda i: (i, 0)),
                  tab_spec],
        out_specs=pl.BlockSpec((tm, slots), lambda i: (i, 0)),
        scratch_shapes=[pltpu.VMEM((4, chunks * TILE_STRIDE, 128), jnp.uint32),
                        pltpu.VMEM((2 * slots, 2 * tm), F32)],
        compiler_params=_cparams(1),
        name="peer_u",
    )(idx_rows, hn, gates, table)


def _peer_v_kernel(idx_ref, w_ref, x1_ref, tab_ref, x2_ref, tiles_ref, lhs_ref, *, chunks):
    tm = GATHER_TOKENS
    slots = w_ref.shape[1]
    half = chunks * 128
    r = lax.broadcasted_iota(jnp.int32, (slots, 2 * slots), 0)
    cc = lax.broadcasted_iota(jnp.int32, (slots, 2 * slots), 1)
    even = jnp.where(cc == 2 * r, 1.0, 0.0).astype(BF16)
    odd = jnp.where(cc == 2 * r + 1, 1.0, 0.0).astype(BF16)
    w_hi, w_lo = _split_bf16(w_ref[...])
    lhs_ref[0] = _dot(w_hi, even)
    lhs_ref[1] = _dot(w_hi, odd)
    lhs_ref[2] = _dot(w_lo, even)
    lhs_ref[3] = _dot(w_lo, odd)
    x2_ref[...] = x1_ref[...]
    sub = lax.broadcasted_iota(jnp.int32, (8, 128), 0)

    def gather(t, tile):
        _gather_rows(idx_ref, tab_ref, tile, t, slots, chunks)

    def compute(t, tile):
        g8 = pl.multiple_of((t >> 3) << 3, 8)
        lhs = jnp.concatenate([lhs_ref[v, pl.ds(g8, 8), :] for v in range(4)],
                              axis=0).astype(BF16)
        mine = sub == (t & 7)
        for c in range(chunks):
            res = _dot(lhs, _tile_chunk(tile, c, slots))
            lo_feat = jnp.where(mine, res[0:8] + res[16:24], 0.0)
            hi_feat = jnp.where(mine, res[8:16] + res[24:32], 0.0)
            x2_ref[pl.ds(g8, 8), c * 128:(c + 1) * 128] += lo_feat
            x2_ref[pl.ds(g8, 8), half + c * 128:half + (c + 1) * 128] += hi_feat

    _for_tokens_pipelined(tm, tiles_ref, gather, compute)


def _peer_v(idx_rows, w, x1, table, chunks, tm=GATHER_TOKENS):
    t, d = x1.shape
    slots = w.shape[1]
    idx_spec, tab_spec = _gather_specs(tm, slots, table)
    return pl.pallas_call(
        functools.partial(_peer_v_kernel, chunks=chunks),
        out_shape=jax.ShapeDtypeStruct((t, d), F32),
        grid=(t // tm,),
        in_specs=[idx_spec,
                  pl.BlockSpec((tm, slots), lambda i: (i, 0)),
                  pl.BlockSpec((tm, d), lambda i: (i, 0)),
                  tab_spec],
        out_specs=pl.BlockSpec((tm, d), lambda i: (i, 0)),
        scratch_shapes=[pltpu.VMEM((4, chunks * TILE_STRIDE, 128), jnp.uint32),
                        pltpu.VMEM((4, tm, 2 * slots), F32)],
        compiler_params=_cparams(1),
        name="peer_v",
    )(idx_rows, w, x1, table)


def _ple_kernel(x_ref, p_ref, gp_ref, wg_ref, wp_ref, gf_ref, o_ref, *, final_norm):
    x = x_ref[...]
    gate = jax.nn.sigmoid(_dot((_rms(x) * gp_ref[...]).astype(BF16), wg_ref[...]))
    x = x + gate * _dot(p_ref[...].astype(BF16), wp_ref[...])
    o_ref[...] = _rms(x) * gf_ref[...] if final_norm else x


def _ple(x2, p2, gp, wg, wp, gf, final_norm, tm=512):
    t, d = x2.shape
    pd = p2.shape[1]
    return pl.pallas_call(
        functools.partial(_ple_kernel, final_norm=final_norm),
        out_shape=jax.ShapeDtypeStruct((t, d), F32),
        grid=(t // tm,),
        in_specs=[pl.BlockSpec((tm, d), lambda i: (i, 0)),
                  pl.BlockSpec((tm, pd), lambda i: (i, 0)),
                  pl.BlockSpec((1, d), lambda i: (0, 0)),
                  pl.BlockSpec((d, d), lambda i: (0, 0)),
                  pl.BlockSpec((pd, d), lambda i: (0, 0)),
                  pl.BlockSpec((1, d), lambda i: (0, 0))],
        out_specs=pl.BlockSpec((tm, d), lambda i: (i, 0)),
        compiler_params=_cparams(1),
        name="ple",
    )(x2, p2, gp.reshape(1, d), wg, wp, gf.reshape(1, d))


def kernel(x, p, norm_mix, w_in, w_att_up, w_pool_group, pool_scale, w_out, norm_ffn,
           w_query, sub_keys, expert_u, expert_v, norm_ple, w_ple_gate, w_ple, norm_final):
    batch, seq, d = x.shape
    depth = w_in.shape[0]
    heads, head_dim = w_att_up.shape[1], w_att_up.shape[2]
    pool = w_pool_group.shape[1] * w_pool_group.shape[2]
    peer_heads = sub_keys.shape[1]
    xt = x.reshape(batch * seq, d)
    for i in range(depth):
        q, k, v, u, ga, gp = _in_proj(xt, norm_mix[i], w_in[i].astype(BF16),
                                      heads * head_dim, head_dim, pool)
        att = _attention(q, k, v, batch, seq, head_dim)
        x1, hn = _mix(xt, att, u, ga, gp, _pair_block_diag(w_att_up[i]).astype(BF16),
                      w_pool_group[i].astype(BF16), pool_scale[i], w_out[i].astype(BF16),
                      norm_ffn[i], batch, seq)
        idx_t, gate_t = _route(hn, w_query[i].astype(BF16), sub_keys[i].astype(BF16),
                               peer_heads)
        chunks = d // 256
        idx_rows = idx_t.T * chunks
        gates = gate_t.T
        w = _peer_u(idx_rows, hn, gates, _pack_table(expert_u[i]), chunks)
        x2 = _peer_v(idx_rows, w, x1, _pack_table(expert_v[i]), chunks)
        xt = _ple(x2, p[i].reshape(batch * seq, -1), norm_ple[i],
                  w_ple_gate[i].astype(BF16), w_ple[i].astype(BF16), norm_final,
                  final_norm=(i == depth - 1))
    return xt.reshape(batch, seq, d)
```

```python
import functools
import math

import jax
import jax.numpy as jnp
from jax import lax
from jax.experimental import pallas as pl
from jax.experimental.pallas import tpu as pltpu

F32 = jnp.float32
BF16 = jnp.bfloat16

EPS = 1e-6
ATT_BLOCK = 128
SURVIVAL_UNDERFLOW = 150.0
POOL_WINDOWS = (2, 4, 8, 16)
PEER_TOPK = 16
N_KEYS = 128
V7X_VMEM_BYTES = 64 * 1024 * 1024
VMEM_LIMIT = V7X_VMEM_BYTES - 8 * 1024 * 1024
GATHER_TOKENS = 64
TILE_STRIDE = 136
INDEX_BITS = 24
NEG_INF = float("-inf")
POS_SENTINEL = 1e9


def _cparams(n_axes):
    return pltpu.CompilerParams(
        dimension_semantics=("arbitrary",) * n_axes, vmem_limit_bytes=VMEM_LIMIT)


def _rms(x):
    return x * lax.rsqrt(jnp.mean(x * x, axis=-1, keepdims=True) + EPS)


def _dot(a, b):
    return jnp.dot(a, b, preferred_element_type=F32)


def _dot_nt(a, b):
    return lax.dot_general(a, b, (((1,), (1,)), ((), ())), preferred_element_type=F32)


def _split_bf16(x):
    hi = x.astype(BF16)
    lo = (x - hi.astype(F32)).astype(BF16)
    return hi, lo


def _in_proj_kernel(x_ref, g_ref, w_ref, q_ref, k_ref, v_ref, u_ref, ga_ref, gp_ref,
                    *, head_dim):
    h = (_rms(x_ref[...]) * g_ref[...]).astype(BF16)
    scale = head_dim ** -0.5
    lo = 0
    for ref, mul in ((q_ref, scale), (k_ref, None), (v_ref, None), (u_ref, None),
                     (ga_ref, None), (gp_ref, None)):
        width = ref.shape[1]
        y = _dot(h, w_ref[:, lo:lo + width])
        ref[...] = (y if mul is None else y * mul).astype(BF16)
        lo += width


def _in_proj(x2, g, w_bf, att, head_dim, pool, tm=512):
    t, d = x2.shape
    n = w_bf.shape[1]
    kern = functools.partial(_in_proj_kernel, head_dim=head_dim)
    widths = (att, att, att, pool, d, d)
    return pl.pallas_call(
        kern,
        out_shape=tuple(jax.ShapeDtypeStruct((t, w), BF16) for w in widths),
        grid=(t // tm,),
        in_specs=[pl.BlockSpec((tm, d), lambda i: (i, 0)),
                  pl.BlockSpec((1, d), lambda i: (0, 0)),
                  pl.BlockSpec((d, n), lambda i: (0, 0))],
        out_specs=tuple(pl.BlockSpec((tm, w), lambda i: (i, 0)) for w in widths),
        compiler_params=_cparams(1),
        name="in_proj",
    )(x2, g.reshape(1, d), w_bf)


def _attn_kernel(q_ref, k_ref, v_ref, o_ref, qm_ref, tri_ref, acc_ref, surv_ref,
                 *, head_dim):
    blk = ATT_BLOCK
    i = pl.program_id(1)
    pairs = q_ref.shape[1] // blk
    lane = lax.broadcasted_iota(jnp.int32, (blk, blk), 1)
    sub = lax.broadcasted_iota(jnp.int32, (blk, blk), 0)
    first = lane < head_dim
    for p in range(pairs):
        qp = q_ref[:, p * blk:(p + 1) * blk]
        zero = jnp.zeros_like(qp)
        qm_ref[p, 0:blk, :] = jnp.where(first, qp, zero)
        qm_ref[p, blk:2 * blk, :] = jnp.where(first, zero, qp)
    r2 = lax.broadcasted_iota(jnp.int32, (2 * blk, 2 * blk), 0) & (blk - 1)
    c2 = lax.broadcasted_iota(jnp.int32, (2 * blk, 2 * blk), 1)
    tri_ref[...] = jnp.where((r2 > c2) | (c2 >= blk), 1.0, 0.0).astype(BF16)
    acc_ref[...] = jnp.zeros_like(acc_ref)
    surv_ref[...] = jnp.zeros_like(surv_ref)
    causal = jnp.concatenate([lane < sub, lane < sub], axis=0)

    def block(j, mask):
        start = pl.multiple_of(j * blk, blk)
        log_beta, parts = [], []
        for p in range(pairs):
            kp = k_ref[pl.ds(start, blk), p * blk:(p + 1) * blk]
            z = _dot_nt(qm_ref[p], kp)
            sp = jnp.maximum(z, 0.0) + jnp.log(1.0 + jnp.exp(-jnp.abs(z)))
            log_beta.append(z - sp)
            if mask is not None:
                sp = jnp.where(mask, sp, 0.0)
            hi, lo = _split_bf16(sp)
            parts.append(jnp.concatenate([hi, lo], axis=1))
        sums = _dot(jnp.concatenate(parts, axis=0), tri_ref[...])
        for p in range(pairs):
            rows = slice(p * 2 * blk, (p + 1) * 2 * blk)
            later, total = sums[rows, :blk], sums[rows, blk:]
            w = jnp.exp(log_beta[p] - later - surv_ref[p])
            if mask is not None:
                w = jnp.where(mask, w, 0.0)
            vp = v_ref[pl.ds(start, blk), p * blk:(p + 1) * blk]
            acc_ref[p] += _dot(w.astype(BF16), vp)
            surv_ref[p] += total

    def any_alive():
        return jnp.min(surv_ref[...]) < SURVIVAL_UNDERFLOW

    block(i, causal)

    def cond(carry):
        return (carry[0] < (i >> 1)) & carry[1]

    def body(carry):
        n = carry[0]
        block(i - 1 - 2 * n, None)
        block(i - 2 - 2 * n, None)
        return n + 1, any_alive()

    _, alive = lax.while_loop(cond, body, (jnp.int32(0), any_alive()))

    @pl.when(((i & 1) == 1) & alive)
    def _():
        block(0, None)

    for p in range(pairs):
        o_ref[:, p * blk:(p + 1) * blk] = jnp.where(
            first, acc_ref[p, 0:blk, :], acc_ref[p, blk:2 * blk, :]).astype(o_ref.dtype)


def _attention(q, k, v, batch, seq, head_dim):
    t, width = q.shape
    blk = ATT_BLOCK
    assert 2 * head_dim == blk and width % blk == 0
    pairs = width // blk
    nq = seq // blk
    qspec = pl.BlockSpec((blk, width), lambda b, i: (b * nq + i, 0))
    kvspec = pl.BlockSpec((seq, width), lambda b, i: (b, 0))
    return pl.pallas_call(
        functools.partial(_attn_kernel, head_dim=head_dim),
        out_shape=jax.ShapeDtypeStruct((t, width), BF16),
        grid=(batch, nq),
        in_specs=[qspec, kvspec, kvspec],
        out_specs=qspec,
        scratch_shapes=[pltpu.VMEM((pairs, 2 * blk, blk), BF16),
                        pltpu.VMEM((2 * blk, 2 * blk), BF16),
                        pltpu.VMEM((pairs, 2 * blk, blk), F32),
                        pltpu.VMEM((pairs, 2 * blk, blk), F32)],
        compiler_params=_cparams(2),
        name="attn",
    )(q, k, v)


def _mix_kernel(x_ref, att_ref, u_ref, ga_ref, gp_ref, wa_ref, wp_ref, ps_ref, wo_ref,
                gn_ref, x1_ref, hn_ref, prev_ref, *, tm):
    s = pl.program_id(1)

    @pl.when(s == 0)
    def _():
        prev_ref[...] = jnp.zeros_like(prev_ref)

    pairs, pw, _ = wa_ref.shape
    a_up = jnp.concatenate(
        [_dot(att_ref[:, p * pw:(p + 1) * pw], wa_ref[p]) for p in range(pairs)],
        axis=1)

    u = u_ref[...]
    ext = jnp.concatenate([prev_ref[...], u], axis=0)
    prev_ref[...] = u
    r = lax.broadcasted_iota(jnp.int32, (tm, 2 * tm), 0) + tm
    c = lax.broadcasted_iota(jnp.int32, (tm, 2 * tm), 1)
    pos = (s * tm + lax.broadcasted_iota(jnp.int32, (tm, 1), 0) + 1).astype(F32)
    groups = len(POOL_WINDOWS)
    gdim = u.shape[1] // groups
    uf = u.astype(F32)
    p_parts = []
    for g, w in enumerate(POOL_WINDOWS):
        band = jnp.where((c <= r) & (c > r - w), 1.0, 0.0).astype(BF16)
        sl = slice(g * gdim, (g + 1) * gdim)
        window_sum = _dot(band, ext[:, sl])
        y = window_sum / jnp.minimum(pos, float(w)) - uf[:, sl]
        p_parts.append(_dot(y.astype(BF16), wp_ref[g]))
    p_up = jnp.concatenate(p_parts, axis=1) * ps_ref[...]

    merged = (jax.nn.sigmoid(ga_ref[...].astype(F32)) * a_up
              + jax.nn.sigmoid(gp_ref[...].astype(F32)) * p_up)
    x1 = x_ref[...] + _dot(merged.astype(BF16), wo_ref[...])
    x1_ref[...] = x1
    hn_ref[...] = (_rms(x1) * gn_ref[...]).astype(BF16)


def _pair_block_diag(w):
    h, a, b = w.shape
    z = jnp.zeros((h // 2, a, b), w.dtype)
    top = jnp.concatenate([w[0::2], z], axis=2)
    bottom = jnp.concatenate([z, w[1::2]], axis=2)
    return jnp.concatenate([top, bottom], axis=1)


def _mix(x2, att, u, ga, gp, wa, wp, ps, wo, gn, batch, seq, tm=256):
    t, d = x2.shape
    att_w = att.shape[1]
    pool = u.shape[1]
    ns = seq // tm
    tok = lambda b, s: (b * ns + s, 0)
    const2 = lambda b, s: (0, 0)
    const3 = lambda b, s: (0, 0, 0)
    kern = functools.partial(_mix_kernel, tm=tm)
    return pl.pallas_call(
        kern,
        out_shape=(jax.ShapeDtypeStruct((t, d), F32), jax.ShapeDtypeStruct((t, d), BF16)),
        grid=(batch, ns),
        in_specs=[pl.BlockSpec((tm, d), tok),
                  pl.BlockSpec((tm, att_w), tok),
                  pl.BlockSpec((tm, pool), tok),
                  pl.BlockSpec((tm, d), tok),
                  pl.BlockSpec((tm, d), tok),
                  pl.BlockSpec(wa.shape, const3),
                  pl.BlockSpec(wp.shape, const3),
                  pl.BlockSpec((1, d), const2),
                  pl.BlockSpec((d, d), const2),
                  pl.BlockSpec((1, d), const2)],
        out_specs=(pl.BlockSpec((tm, d), tok), pl.BlockSpec((tm, d), tok)),
        scratch_shapes=[pltpu.VMEM((tm, pool), BF16)],
        compiler_params=_cparams(2),
        name="mix",
    )(x2, att, u, ga, gp, wa, wp, ps.reshape(1, d), wo, gn.reshape(1, d))


def _topk_rows(s, payload, k, pos=None):
    if pos is None:
        pos = lax.broadcasted_iota(jnp.int32, s.shape, 0).astype(F32)
    vals, poss, pays = [], [], []
    for _ in range(k):
        m = jnp.max(s, axis=0, keepdims=True)
        p = jnp.min(jnp.where(s == m, pos, POS_SENTINEL), axis=0, keepdims=True)
        hit = pos == p
        if payload is not None:
            pays.append(jnp.sum(jnp.where(hit, payload, 0.0), axis=0, keepdims=True))
        s = jnp.where(hit, NEG_INF, s)
        vals.append(m)
        poss.append(p)
    cat = lambda xs: jnp.concatenate(xs, axis=0)
    return cat(vals), cat(poss), (cat(pays) if payload is not None else None)


def _route_kernel(hn_ref, wq_ref, keys_ref, idx_ref, gate_ref, *, heads):
    q = _dot(hn_ref[...], wq_ref[...]).astype(BF16)
    half = keys_ref.shape[-1]
    k = PEER_TOPK
    for hd in range(heads):
        tops = []
        for c in range(2):
            lo = (hd * 2 + c) * half
            scores = _dot_nt(keys_ref[hd, c], q[:, lo:lo + half])
            tops.append(_topk_rows(scores, None, k)[:2])
        (s0, i0), (s1, i1) = tops
        h = k // 2
        groups = ([(slice(0, 1), slice(0, k))]
                  + [(slice(a, a + 1), slice(0, h)) for a in range(1, h)]
                  + [(slice(h, k), slice(0, 1))])
        cand_s = jnp.concatenate([s0[ra, :] + s1[rb, :] for ra, rb in groups], axis=0)
        cand_i = jnp.concatenate(
            [i0[ra, :] * float(N_KEYS) + i1[rb, :] for ra, rb in groups], axis=0)
        n = s0.shape[1]
        row = lambda lo, cnt: (lax.broadcasted_iota(jnp.int32, (cnt, n), 0) + lo).astype(F32)
        cand_pos = jnp.concatenate(
            [row(0, k)] + [row(a * k, h) for a in range(1, h)] + [row(h, k - h) * float(k)],
            axis=0)
        best_s, _, best_i = _topk_rows(cand_s, cand_i, k, cand_pos)
        e = jnp.exp(best_s - best_s[0:1, :])
        gate_ref[hd * k:(hd + 1) * k, :] = e / jnp.sum(e, axis=0, keepdims=True)
        idx_ref[hd * k:(hd + 1) * k, :] = best_i.astype(jnp.int32)


def _route(hn, wq_bf, keys_bf, heads, tm=256):
    t, d = hn.shape
    nq = wq_bf.shape[1]
    slots = heads * PEER_TOPK
    kern = functools.partial(_route_kernel, heads=heads)
    return pl.pallas_call(
        kern,
        out_shape=(jax.ShapeDtypeStruct((slots, t), jnp.int32),
                   jax.ShapeDtypeStruct((slots, t), F32)),
        grid=(t // tm,),
        in_specs=[pl.BlockSpec((tm, d), lambda i: (i, 0)),
                  pl.BlockSpec((d, nq), lambda i: (0, 0)),
                  pl.BlockSpec(keys_bf.shape, lambda i: (0, 0, 0, 0))],
        out_specs=(pl.BlockSpec((slots, tm), lambda i: (0, i)),
                   pl.BlockSpec((slots, tm), lambda i: (0, i))),
        compiler_params=_cparams(1),
        name="route",
    )(hn, wq_bf, keys_bf)


def _pack_table(w):
    e, d = w.shape
    bits = lax.bitcast_convert_type(w.astype(BF16), jnp.uint16).astype(jnp.uint32)
    lo, hi = bits[:, :d // 2], bits[:, d // 2:]
    return (lo | (hi << 16)).reshape(e * (d // 256), 128)


def _gather_rows(idx_ref, tab_ref, tile_ref, t, slots, chunks, after=None):
    if after is not None:
        t = t + lax.shift_right_logical(after, INDEX_BITS)
    row = None
    for k in range(slots):
        row = pl.multiple_of(idx_ref.at[k][t], chunks)
        tile_ref[k * chunks:(k + 1) * chunks, :] = tab_ref[pl.ds(row, chunks), :]
    return row


def _tile_chunk(tile_ref, c, slots):
    chunks = tile_ref.shape[0] // slots
    return pltpu.bitcast(tile_ref[pl.ds(c, slots, stride=chunks), :], BF16)


def _for_tokens_pipelined(tm, tiles, gather, compute, commit):
    a, b, c, d = tiles
    gather(1, b, gather(0, a, None))
    last = tm - 1

    def quad(n, carry):
        t0 = 4 * n
        ra = compute(t0, a, 0)
        rb = compute(t0 + 1, b, 1)
        order = gather(t0 + 2, c, None)
        order = gather(t0 + 3, d, order)
        rc = compute(t0 + 2, c, 2)
        rd = compute(t0 + 3, d, 3)
        order = gather(jnp.minimum(t0 + 4, last), a, order)
        gather(jnp.minimum(t0 + 5, last), b, order)
        commit(t0, (ra, rb, rc, rd))
        return carry

    lax.fori_loop(0, tm // 4, quad, 0)


def _two_tile_step(idx_hbm, idx_smem, sem, process):
    step = pl.program_id(0)

    def index_copy(tile, slot):
        return pltpu.make_async_copy(idx_hbm.at[tile], idx_smem.at[slot], sem.at[slot])

    @pl.when(step == 0)
    def _():
        index_copy(0, 0).start()

    index_copy(2 * step, 0).wait()
    index_copy(2 * step + 1, 1).start()
    process(0, idx_smem.at[0])
    index_copy(2 * step + 1, 1).wait()

    @pl.when(step + 1 < pl.num_programs(0))
    def _():
        index_copy(2 * step + 2, 0).start()

    process(1, idx_smem.at[1])


def _gather_call(kernel_fn, name, idx_rows, table, chunks, blocked_inputs, out_width,
                 extra_scratch):
    n_tiles, slots, tm = idx_rows.shape
    t = n_tiles * tm
    tok = lambda width: pl.BlockSpec((2 * tm, width), lambda i: (i, 0))
    return pl.pallas_call(
        functools.partial(kernel_fn, chunks=chunks),
        out_shape=jax.ShapeDtypeStruct((t, out_width), F32),
        grid=(t // (2 * tm),),
        in_specs=([pl.BlockSpec(memory_space=pl.ANY)]
                  + [tok(x.shape[1]) for x in blocked_inputs]
                  + [pl.BlockSpec(table.shape, lambda i: (0, 0),
                                  pipeline_mode=pl.Buffered(1))]),
        out_specs=tok(out_width),
        scratch_shapes=[pltpu.VMEM((chunks * slots, 128), jnp.uint32)] * 4 + [
                        pltpu.SMEM((2, slots, tm), jnp.int32),
                        pltpu.SemaphoreType.DMA((2,))] + extra_scratch,
        compiler_params=_cparams(1),
        name=name,
    )(idx_rows, *blocked_inputs, table)


def _peer_u_kernel(idx_hbm, hn_ref, gate_ref, tab_ref, w_ref, ta, tb, tc, td, idx_smem, sem,
                   z_ref, *, chunks):
    tm = GATHER_TOKENS
    slots = gate_ref.shape[1]
    half = chunks * 128
    lane = lax.broadcasted_iota(jnp.int32, (2 * slots, 2 * tm), 1)

    def process(part, idx_ref):
        tok = slice(part * tm, (part + 1) * tm)
        hn = hn_ref[tok, :].astype(F32)
        rhs = jnp.concatenate([hn[:, :half], hn[:, half:]], axis=0).T.astype(BF16)
        z_ref[...] = jnp.zeros_like(z_ref)

        def gather(t, tile, after):
            return _gather_rows(idx_ref, tab_ref, tile, t, slots, chunks, after)

        def compute(t, tile, j):
            rows = jnp.concatenate([_tile_chunk(tile, c, slots) for c in range(chunks)],
                                   axis=1)
            y = _dot(rows, rhs)
            return jnp.where((lane == t) | (lane == t + tm), y, 0.0)

        def commit(t0, parts):
            z_ref[...] += (parts[0] + parts[1]) + (parts[2] + parts[3])

        _for_tokens_pipelined(tm, (ta, tb, tc, td), gather, compute, commit)
        act_t = (z_ref[pl.ds(0, slots, stride=2), :][:, 0:tm]
                 + z_ref[pl.ds(1, slots, stride=2), :][:, tm:2 * tm])
        pre = act_t.T
        act = 0.5 * pre * (1.0 + lax.erf(pre * math.sqrt(0.5)))
        w_ref[tok, :] = gate_ref[tok, :] * act

    _two_tile_step(idx_hbm, idx_smem, sem, process)


def _peer_u(idx_rows, hn, gates, table, chunks):
    slots = gates.shape[1]
    return _gather_call(_peer_u_kernel, "peer_u", idx_rows, table, chunks, [hn, gates],
                        slots, [pltpu.VMEM((2 * slots, 2 * GATHER_TOKENS), F32)])


def _peer_v_kernel(idx_hbm, w_ref, x1_ref, tab_ref, x2_ref, ta, tb, tc, td, idx_smem, sem,
                   lhs_ref, *, chunks):
    tm = GATHER_TOKENS
    slots = w_ref.shape[1]
    half = chunks * 128
    r = lax.broadcasted_iota(jnp.int32, (slots, 2 * slots), 0)
    cc = lax.broadcasted_iota(jnp.int32, (slots, 2 * slots), 1)
    even = jnp.where(cc == 2 * r, 1.0, 0.0).astype(BF16)
    odd = jnp.where(cc == 2 * r + 1, 1.0, 0.0).astype(BF16)
    w_hi, w_lo = _split_bf16(w_ref[...])
    lhs_ref[0] = _dot(w_hi, even)
    lhs_ref[1] = _dot(w_hi, odd)
    lhs_ref[2] = _dot(w_lo, even)
    lhs_ref[3] = _dot(w_lo, odd)
    x2_ref[...] = x1_ref[...]
    sub = lax.broadcasted_iota(jnp.int32, (8, 128), 0)

    def process(part, idx_ref):
        def gather(t, tile, after):
            return _gather_rows(idx_ref, tab_ref, tile, t, slots, chunks, after)

        def rows8(t):
            return pl.multiple_of(part * tm + ((t >> 3) << 3), 8)

        def compute(t, tile, j):
            g8 = rows8(t)
            lhs = jnp.concatenate([lhs_ref[v, pl.ds(g8, 8), :] for v in range(4)],
                                  axis=0).astype(BF16)
            mine = sub == (t & 7)
            lo_feat, hi_feat = [], []
            for c in range(chunks):
                res = _dot(lhs, _tile_chunk(tile, c, slots))
                lo_feat.append(jnp.where(mine, res[0:8] + res[16:24], 0.0))
                hi_feat.append(jnp.where(mine, res[8:16] + res[24:32], 0.0))
            return jnp.concatenate(lo_feat + hi_feat, axis=1)

        def commit(t0, parts):
            x2_ref[pl.ds(rows8(t0), 8), :] += (parts[0] + parts[1]) + (parts[2] + parts[3])

        _for_tokens_pipelined(tm, (ta, tb, tc, td), gather, compute, commit)

    _two_tile_step(idx_hbm, idx_smem, sem, process)


def _peer_v(idx_rows, w, x1, table, chunks):
    slots = w.shape[1]
    return _gather_call(_peer_v_kernel, "peer_v", idx_rows, table, chunks, [w, x1],
                        x1.shape[1], [pltpu.VMEM((4, 2 * GATHER_TOKENS, 2 * slots), F32)])


def _ple_kernel(x_ref, p_ref, gp_ref, wg_ref, wp_ref, gf_ref, o_ref, *, final_norm):
    x = x_ref[...]
    gate = jax.nn.sigmoid(_dot((_rms(x) * gp_ref[...]).astype(BF16), wg_ref[...]))
    x = x + gate * _dot(p_ref[...].astype(BF16), wp_ref[...])
    o_ref[...] = _rms(x) * gf_ref[...] if final_norm else x


def _ple(x2, p2, gp, wg, wp, gf, final_norm, tm=512):
    t, d = x2.shape
    pd = p2.shape[1]
    return pl.pallas_call(
        functools.partial(_ple_kernel, final_norm=final_norm),
        out_shape=jax.ShapeDtypeStruct((t, d), F32),
        grid=(t // tm,),
        in_specs=[pl.BlockSpec((tm, d), lambda i: (i, 0)),
                  pl.BlockSpec((tm, pd), lambda i: (i, 0)),
                  pl.BlockSpec((1, d), lambda i: (0, 0)),
                  pl.BlockSpec((d, d), lambda i: (0, 0)),
                  pl.BlockSpec((pd, d), lambda i: (0, 0)),
                  pl.BlockSpec((1, d), lambda i: (0, 0))],
        out_specs=pl.BlockSpec((tm, d), lambda i: (i, 0)),
        compiler_params=_cparams(1),
        name="ple",
    )(x2, p2, gp.reshape(1, d), wg, wp, gf.reshape(1, d))


def kernel(x, p, norm_mix, w_in, w_att_up, w_pool_group, pool_scale, w_out, norm_ffn,
           w_query, sub_keys, expert_u, expert_v, norm_ple, w_ple_gate, w_ple, norm_final):
    batch, seq, d = x.shape
    depth = w_in.shape[0]
    heads, head_dim = w_att_up.shape[1], w_att_up.shape[2]
    pool = w_pool_group.shape[1] * w_pool_group.shape[2]
    peer_heads = sub_keys.shape[1]
    xt = x.reshape(batch * seq, d)
    for i in range(depth):
        q, k, v, u, ga, gp = _in_proj(xt, norm_mix[i], w_in[i].astype(BF16),
                                      heads * head_dim, head_dim, pool)
        att = _attention(q, k, v, batch, seq, head_dim)
        x1, hn = _mix(xt, att, u, ga, gp, _pair_block_diag(w_att_up[i]).astype(BF16),
                      w_pool_group[i].astype(BF16), pool_scale[i], w_out[i].astype(BF16),
                      norm_ffn[i], batch, seq)
        idx_t, gate_t = _route(hn, w_query[i].astype(BF16), sub_keys[i].astype(BF16),
                               peer_heads)
        chunks = d // 256
        idx_rows = (idx_t * chunks).reshape(idx_t.shape[0], -1, GATHER_TOKENS).swapaxes(0, 1)
        gates = gate_t.T
        w = _peer_u(idx_rows, hn, gates, _pack_table(expert_u[i]), chunks)
        x2 = _peer_v(idx_rows, w, x1, _pack_table(expert_v[i]), chunks)
        xt = _ple(x2, p[i].reshape(batch * seq, -1), norm_ple[i],
                  w_ple_gate[i].astype(BF16), w_ple[i].astype(BF16), norm_final,
                  final_norm=(i == depth - 1))
    return xt.reshape(batch, seq, d)
```

```python
import functools
import math

import jax
import jax.numpy as jnp
from jax import lax
from jax.experimental import pallas as pl
from jax.experimental.pallas import tpu as pltpu

F32 = jnp.float32
BF16 = jnp.bfloat16

EPS = 1e-6
ATT_BLOCK = 128
SURVIVAL_UNDERFLOW = 150.0
POOL_WINDOWS = (2, 4, 8, 16)
PEER_TOPK = 16
N_KEYS = 128
V7X_VMEM_BYTES = 64 * 1024 * 1024
VMEM_LIMIT = V7X_VMEM_BYTES - 8 * 1024 * 1024
GATHER_TOKENS = 64
NEG_INF = float("-inf")
POS_SENTINEL = 1e9


def _cparams(n_axes):
    return pltpu.CompilerParams(
        dimension_semantics=("arbitrary",) * n_axes, vmem_limit_bytes=VMEM_LIMIT)


def _rms(x):
    return x * lax.rsqrt(jnp.mean(x * x, axis=-1, keepdims=True) + EPS)


def _dot(a, b):
    return jnp.dot(a, b, preferred_element_type=F32)


def _dot_nt(a, b):
    return lax.dot_general(a, b, (((1,), (1,)), ((), ())), preferred_element_type=F32)


def _split_bf16(x):
    hi = x.astype(BF16)
    lo = (x - hi.astype(F32)).astype(BF16)
    return hi, lo


def _in_proj_kernel(x_ref, g_ref, w_ref, q_ref, k_ref, v_ref, u_ref, ga_ref, gp_ref,
                    *, head_dim):
    h = (_rms(x_ref[...]) * g_ref[...]).astype(BF16)
    scale = head_dim ** -0.5
    lo = 0
    for ref, mul in ((q_ref, scale), (k_ref, None), (v_ref, None), (u_ref, None),
                     (ga_ref, None), (gp_ref, None)):
        width = ref.shape[1]
        y = _dot(h, w_ref[:, lo:lo + width])
        ref[...] = (y if mul is None else y * mul).astype(BF16)
        lo += width


def _in_proj(x2, g, w_bf, att, head_dim, pool, tm=512):
    t, d = x2.shape
    n = w_bf.shape[1]
    kern = functools.partial(_in_proj_kernel, head_dim=head_dim)
    widths = (att, att, att, pool, d, d)
    return pl.pallas_call(
        kern,
        out_shape=tuple(jax.ShapeDtypeStruct((t, w), BF16) for w in widths),
        grid=(t // tm,),
        in_specs=[pl.BlockSpec((tm, d), lambda i: (i, 0)),
                  pl.BlockSpec((1, d), lambda i: (0, 0)),
                  pl.BlockSpec((d, n), lambda i: (0, 0))],
        out_specs=tuple(pl.BlockSpec((tm, w), lambda i: (i, 0)) for w in widths),
        compiler_params=_cparams(1),
        name="in_proj",
    )(x2, g.reshape(1, d), w_bf)


def _attn_kernel(q_ref, k_ref, v_ref, o_ref, qm_ref, tri_ref, acc_ref, surv_ref,
                 *, head_dim):
    blk = ATT_BLOCK
    i = pl.program_id(1)
    pairs = q_ref.shape[1] // blk
    lane = lax.broadcasted_iota(jnp.int32, (blk, blk), 1)
    sub = lax.broadcasted_iota(jnp.int32, (blk, blk), 0)
    first = lane < head_dim
    for p in range(pairs):
        qp = q_ref[:, p * blk:(p + 1) * blk]
        zero = jnp.zeros_like(qp)
        qm_ref[p, 0:blk, :] = jnp.where(first, qp, zero)
        qm_ref[p, blk:2 * blk, :] = jnp.where(first, zero, qp)
    r2 = lax.broadcasted_iota(jnp.int32, (2 * blk, 2 * blk), 0) & (blk - 1)
    c2 = lax.broadcasted_iota(jnp.int32, (2 * blk, 2 * blk), 1)
    tri_ref[...] = jnp.where((r2 > c2) | (c2 >= blk), 1.0, 0.0).astype(BF16)
    acc_ref[...] = jnp.zeros_like(acc_ref)
    surv_ref[...] = jnp.zeros_like(surv_ref)
    causal = jnp.concatenate([lane < sub, lane < sub], axis=0)

    def block(j, mask):
        start = pl.multiple_of(j * blk, blk)
        log_beta, parts = [], []
        for p in range(pairs):
            kp = k_ref[pl.ds(start, blk), p * blk:(p + 1) * blk]
            z = _dot_nt(qm_ref[p], kp)
            sp = jnp.maximum(z, 0.0) + jnp.log(1.0 + jnp.exp(-jnp.abs(z)))
            log_beta.append(z - sp)
            if mask is not None:
                sp = jnp.where(mask, sp, 0.0)
            hi, lo = _split_bf16(sp)
            parts.append(jnp.concatenate([hi, lo], axis=1))
        sums = _dot(jnp.concatenate(parts, axis=0), tri_ref[...])
        for p in range(pairs):
            rows = slice(p * 2 * blk, (p + 1) * 2 * blk)
            later, total = sums[rows, :blk], sums[rows, blk:]
            w = jnp.exp(log_beta[p] - later - surv_ref[p])
            if mask is not None:
                w = jnp.where(mask, w, 0.0)
            vp = v_ref[pl.ds(start, blk), p * blk:(p + 1) * blk]
            acc_ref[p] += _dot(w.astype(BF16), vp)
            surv_ref[p] += total

    def any_alive():
        return jnp.min(surv_ref[...]) < SURVIVAL_UNDERFLOW

    block(i, causal)

    def cond(carry):
        return (carry[0] < (i >> 1)) & carry[1]

    def body(carry):
        n = carry[0]
        block(i - 1 - 2 * n, None)
        block(i - 2 - 2 * n, None)
        return n + 1, any_alive()

    _, alive = lax.while_loop(cond, body, (jnp.int32(0), any_alive()))

    @pl.when(((i & 1) == 1) & alive)
    def _():
        block(0, None)

    for p in range(pairs):
        o_ref[:, p * blk:(p + 1) * blk] = jnp.where(
            first, acc_ref[p, 0:blk, :], acc_ref[p, blk:2 * blk, :]).astype(o_ref.dtype)


def _attention(q, k, v, batch, seq, head_dim):
    t, width = q.shape
    blk = ATT_BLOCK
    assert 2 * head_dim == blk and width % blk == 0
    pairs = width // blk
    nq = seq // blk
    qspec = pl.BlockSpec((blk, width), lambda b, i: (b * nq + i, 0))
    kvspec = pl.BlockSpec((seq, width), lambda b, i: (b, 0))
    return pl.pallas_call(
        functools.partial(_attn_kernel, head_dim=head_dim),
        out_shape=jax.ShapeDtypeStruct((t, width), BF16),
        grid=(batch, nq),
        in_specs=[qspec, kvspec, kvspec],
        out_specs=qspec,
        scratch_shapes=[pltpu.VMEM((pairs, 2 * blk, blk), BF16),
                        pltpu.VMEM((2 * blk, 2 * blk), BF16),
                        pltpu.VMEM((pairs, 2 * blk, blk), F32),
                        pltpu.VMEM((pairs, 2 * blk, blk), F32)],
        compiler_params=_cparams(2),
        name="attn",
    )(q, k, v)


def _mix_kernel(x_ref, att_ref, u_ref, ga_ref, gp_ref, wa_ref, wp_ref, ps_ref, wo_ref,
                gn_ref, x1_ref, hn_ref, prev_ref, *, tm):
    s = pl.program_id(1)

    @pl.when(s == 0)
    def _():
        prev_ref[...] = jnp.zeros_like(prev_ref)

    pairs, pw, _ = wa_ref.shape
    a_up = jnp.concatenate(
        [_dot(att_ref[:, p * pw:(p + 1) * pw], wa_ref[p]) for p in range(pairs)],
        axis=1)

    u = u_ref[...]
    ext = jnp.concatenate([prev_ref[...], u], axis=0)
    prev_ref[...] = u
    r = lax.broadcasted_iota(jnp.int32, (tm, 2 * tm), 0) + tm
    c = lax.broadcasted_iota(jnp.int32, (tm, 2 * tm), 1)
    pos = (s * tm + lax.broadcasted_iota(jnp.int32, (tm, 1), 0) + 1).astype(F32)
    groups = len(POOL_WINDOWS)
    gdim = u.shape[1] // groups
    uf = u.astype(F32)
    p_parts = []
    for g, w in enumerate(POOL_WINDOWS):
        band = jnp.where((c <= r) & (c > r - w), 1.0, 0.0).astype(BF16)
        sl = slice(g * gdim, (g + 1) * gdim)
        window_sum = _dot(band, ext[:, sl])
        y = window_sum / jnp.minimum(pos, float(w)) - uf[:, sl]
        p_parts.append(_dot(y.astype(BF16), wp_ref[g]))
    p_up = jnp.concatenate(p_parts, axis=1) * ps_ref[...]

    merged = (jax.nn.sigmoid(ga_ref[...].astype(F32)) * a_up
              + jax.nn.sigmoid(gp_ref[...].astype(F32)) * p_up)
    x1 = x_ref[...] + _dot(merged.astype(BF16), wo_ref[...])
    x1_ref[...] = x1
    hn_ref[...] = (_rms(x1) * gn_ref[...]).astype(BF16)


def _pair_block_diag(w):
    h, a, b = w.shape
    z = jnp.zeros((h // 2, a, b), w.dtype)
    top = jnp.concatenate([w[0::2], z], axis=2)
    bottom = jnp.concatenate([z, w[1::2]], axis=2)
    return jnp.concatenate([top, bottom], axis=1)


def _mix(x2, att, u, ga, gp, wa, wp, ps, wo, gn, batch, seq, tm=256):
    t, d = x2.shape
    att_w = att.shape[1]
    pool = u.shape[1]
    ns = seq // tm
    tok = lambda b, s: (b * ns + s, 0)
    const2 = lambda b, s: (0, 0)
    const3 = lambda b, s: (0, 0, 0)
    kern = functools.partial(_mix_kernel, tm=tm)
    return pl.pallas_call(
        kern,
        out_shape=(jax.ShapeDtypeStruct((t, d), F32), jax.ShapeDtypeStruct((t, d), BF16)),
        grid=(batch, ns),
        in_specs=[pl.BlockSpec((tm, d), tok),
                  pl.BlockSpec((tm, att_w), tok),
                  pl.BlockSpec((tm, pool), tok),
                  pl.BlockSpec((tm, d), tok),
                  pl.BlockSpec((tm, d), tok),
                  pl.BlockSpec(wa.shape, const3),
                  pl.BlockSpec(wp.shape, const3),
                  pl.BlockSpec((1, d), const2),
                  pl.BlockSpec((d, d), const2),
                  pl.BlockSpec((1, d), const2)],
        out_specs=(pl.BlockSpec((tm, d), tok), pl.BlockSpec((tm, d), tok)),
        scratch_shapes=[pltpu.VMEM((tm, pool), BF16)],
        compiler_params=_cparams(2),
        name="mix",
    )(x2, att, u, ga, gp, wa, wp, ps.reshape(1, d), wo, gn.reshape(1, d))


def _topk_rows(s, payload, k, pos=None):
    if pos is None:
        pos = lax.broadcasted_iota(jnp.int32, s.shape, 0).astype(F32)
    vals, poss, pays = [], [], []
    for _ in range(k):
        m = jnp.max(s, axis=0, keepdims=True)
        p = jnp.min(jnp.where(s == m, pos, POS_SENTINEL), axis=0, keepdims=True)
        hit = pos == p
        if payload is not None:
            pays.append(jnp.sum(jnp.where(hit, payload, 0.0), axis=0, keepdims=True))
        s = jnp.where(hit, NEG_INF, s)
        vals.append(m)
        poss.append(p)
    cat = lambda xs: jnp.concatenate(xs, axis=0)
    return cat(vals), cat(poss), (cat(pays) if payload is not None else None)


def _route_kernel(hn_ref, wq_ref, keys_ref, idx_ref, gate_ref, *, heads):
    q = _dot(hn_ref[...], wq_ref[...]).astype(BF16)
    half = keys_ref.shape[-1]
    k = PEER_TOPK
    for hd in range(heads):
        tops = []
        for c in range(2):
            lo = (hd * 2 + c) * half
            scores = _dot_nt(keys_ref[hd, c], q[:, lo:lo + half])
            tops.append(_topk_rows(scores, None, k)[:2])
        (s0, i0), (s1, i1) = tops
        h = k // 2
        groups = ([(slice(0, 1), slice(0, k))]
                  + [(slice(a, a + 1), slice(0, h)) for a in range(1, h)]
                  + [(slice(h, k), slice(0, 1))])
        cand_s = jnp.concatenate([s0[ra, :] + s1[rb, :] for ra, rb in groups], axis=0)
        cand_i = jnp.concatenate(
            [i0[ra, :] * float(N_KEYS) + i1[rb, :] for ra, rb in groups], axis=0)
        n = s0.shape[1]
        row = lambda lo, cnt: (lax.broadcasted_iota(jnp.int32, (cnt, n), 0) + lo).astype(F32)
        cand_pos = jnp.concatenate(
            [row(0, k)] + [row(a * k, h) for a in range(1, h)] + [row(h, k - h) * float(k)],
            axis=0)
        best_s, _, best_i = _topk_rows(cand_s, cand_i, k, cand_pos)
        e = jnp.exp(best_s - best_s[0:1, :])
        gate_ref[hd * k:(hd + 1) * k, :] = e / jnp.sum(e, axis=0, keepdims=True)
        idx_ref[hd * k:(hd + 1) * k, :] = best_i.astype(jnp.int32)


def _route(hn, wq_bf, keys_bf, heads, tm=256):
    t, d = hn.shape
    nq = wq_bf.shape[1]
    slots = heads * PEER_TOPK
    kern = functools.partial(_route_kernel, heads=heads)
    return pl.pallas_call(
        kern,
        out_shape=(jax.ShapeDtypeStruct((slots, t), jnp.int32),
                   jax.ShapeDtypeStruct((slots, t), F32)),
        grid=(t // tm,),
        in_specs=[pl.BlockSpec((tm, d), lambda i: (i, 0)),
                  pl.BlockSpec((d, nq), lambda i: (0, 0)),
                  pl.BlockSpec(keys_bf.shape, lambda i: (0, 0, 0, 0))],
        out_specs=(pl.BlockSpec((slots, tm), lambda i: (0, i)),
                   pl.BlockSpec((slots, tm), lambda i: (0, i))),
        compiler_params=_cparams(1),
        name="route",
    )(hn, wq_bf, keys_bf)


def _pack_table(w):
    e, d = w.shape
    bits = lax.bitcast_convert_type(w.astype(BF16), jnp.uint16).astype(jnp.uint32)
    lo, hi = bits[:, :d // 2], bits[:, d // 2:]
    return (lo | (hi << 16)).reshape(e * (d // 256), 128)


def _gather_rows(idx_ref, tab_ref, tile_ref, t, slots, chunks):
    for k in range(slots):
        row = pl.multiple_of(idx_ref.at[k][t], chunks)
        tile_ref[k * chunks:(k + 1) * chunks, :] = tab_ref[pl.ds(row, chunks), :]


def _tile_chunk(tile_ref, c, slots):
    chunks = tile_ref.shape[0] // slots
    return pltpu.bitcast(tile_ref[pl.ds(c, slots, stride=chunks), :], BF16)


def _for_tokens_pipelined(tm, tiles, gather, compute, commit):
    a, b, c, d = tiles
    gather(0, a)
    gather(1, b)
    last = tm - 1

    def four_tokens(t0):
        ra = compute(t0, a)
        rb = compute(t0 + 1, b)
        gather(t0 + 2, c)
        gather(t0 + 3, d)
        rc = compute(t0 + 2, c)
        rd = compute(t0 + 3, d)
        gather(jnp.minimum(t0 + 4, last), a)
        gather(jnp.minimum(t0 + 5, last), b)
        commit(t0, (ra, rb, rc, rd))

    def trip(n, carry):
        four_tokens(8 * n)
        four_tokens(8 * n + 4)
        return carry

    lax.fori_loop(0, tm // 8, trip, 0)


def _two_tile_step(idx_hbm, idx_smem, sem, process):
    step = pl.program_id(0)

    def index_copy(tile, slot):
        return pltpu.make_async_copy(idx_hbm.at[tile], idx_smem.at[slot], sem.at[slot])

    @pl.when(step == 0)
    def _():
        index_copy(0, 0).start()

    index_copy(2 * step, 0).wait()
    index_copy(2 * step + 1, 1).start()
    process(0, idx_smem.at[0])
    index_copy(2 * step + 1, 1).wait()

    @pl.when(step + 1 < pl.num_programs(0))
    def _():
        index_copy(2 * step + 2, 0).start()

    process(1, idx_smem.at[1])


def _gather_call(kernel_fn, name, idx_rows, table, chunks, blocked_inputs, out_width,
                 extra_scratch):
    n_tiles, slots, tm = idx_rows.shape
    t = n_tiles * tm
    tok = lambda width: pl.BlockSpec((2 * tm, width), lambda i: (i, 0))
    return pl.pallas_call(
        functools.partial(kernel_fn, chunks=chunks),
        out_shape=jax.ShapeDtypeStruct((t, out_width), F32),
        grid=(t // (2 * tm),),
        in_specs=([pl.BlockSpec(memory_space=pl.ANY)]
                  + [tok(x.shape[1]) for x in blocked_inputs]
                  + [pl.BlockSpec(table.shape, lambda i: (0, 0),
                                  pipeline_mode=pl.Buffered(1))]),
        out_specs=tok(out_width),
        scratch_shapes=[pltpu.VMEM((chunks * slots, 128), jnp.uint32)] * 4 + [
                        pltpu.SMEM((2, slots, tm), jnp.int32),
                        pltpu.SemaphoreType.DMA((2,))] + extra_scratch,
        compiler_params=_cparams(1),
        name=name,
    )(idx_rows, *blocked_inputs, table)


def _peer_u_kernel(idx_hbm, hn_ref, gate_ref, tab_ref, w_ref, ta, tb, tc, td, idx_smem, sem,
                   z_ref, *, chunks):
    tm = GATHER_TOKENS
    slots = gate_ref.shape[1]
    half = chunks * 128
    lane = lax.broadcasted_iota(jnp.int32, (2 * slots, 2 * tm), 1)

    def process(part, idx_ref):
        tok = slice(part * tm, (part + 1) * tm)
        hn = hn_ref[tok, :].astype(F32)
        rhs = jnp.concatenate([hn[:, :half], hn[:, half:]], axis=0).T.astype(BF16)
        z_ref[...] = jnp.zeros_like(z_ref)

        def gather(t, tile):
            _gather_rows(idx_ref, tab_ref, tile, t, slots, chunks)

        def compute(t, tile):
            rows = jnp.concatenate([_tile_chunk(tile, c, slots) for c in range(chunks)],
                                   axis=1)
            y = _dot(rows, rhs)
            return jnp.where((lane == t) | (lane == t + tm), y, 0.0)

        def commit(t0, parts):
            z_ref[...] += (parts[0] + parts[1]) + (parts[2] + parts[3])

        _for_tokens_pipelined(tm, (ta, tb, tc, td), gather, compute, commit)
        act_t = (z_ref[pl.ds(0, slots, stride=2), :][:, 0:tm]
                 + z_ref[pl.ds(1, slots, stride=2), :][:, tm:2 * tm])
        pre = act_t.T
        act = 0.5 * pre * (1.0 + lax.erf(pre * math.sqrt(0.5)))
        w_ref[tok, :] = gate_ref[tok, :] * act

    _two_tile_step(idx_hbm, idx_smem, sem, process)


def _peer_u(idx_rows, hn, gates, table, chunks):
    slots = gates.shape[1]
    return _gather_call(_peer_u_kernel, "peer_u", idx_rows, table, chunks, [hn, gates],
                        slots, [pltpu.VMEM((2 * slots, 2 * GATHER_TOKENS), F32)])


def _peer_v_kernel(idx_hbm, w_ref, x1_ref, tab_ref, x2_ref, ta, tb, tc, td, idx_smem, sem,
                   lhs_ref, *, chunks):
    tm = GATHER_TOKENS
    slots = w_ref.shape[1]
    half = chunks * 128
    r = lax.broadcasted_iota(jnp.int32, (slots, 2 * slots), 0)
    cc = lax.broadcasted_iota(jnp.int32, (slots, 2 * slots), 1)
    even = jnp.where(cc == 2 * r, 1.0, 0.0).astype(BF16)
    odd = jnp.where(cc == 2 * r + 1, 1.0, 0.0).astype(BF16)
    w_hi, w_lo = _split_bf16(w_ref[...])
    lhs_ref[0] = _dot(w_hi, even)
    lhs_ref[1] = _dot(w_hi, odd)
    lhs_ref[2] = _dot(w_lo, even)
    lhs_ref[3] = _dot(w_lo, odd)
    x2_ref[...] = x1_ref[...]
    sub = lax.broadcasted_iota(jnp.int32, (8, 128), 0)

    def process(part, idx_ref):
        def gather(t, tile):
            _gather_rows(idx_ref, tab_ref, tile, t, slots, chunks)

        def rows8(t):
            return pl.multiple_of(part * tm + ((t >> 3) << 3), 8)

        def compute(t, tile):
            g8 = rows8(t)
            lhs = jnp.concatenate([lhs_ref[v, pl.ds(g8, 8), :] for v in range(4)],
                                  axis=0).astype(BF16)
            mine = sub == (t & 7)
            lo_feat, hi_feat = [], []
            for c in range(chunks):
                res = _dot(lhs, _tile_chunk(tile, c, slots))
                lo_feat.append(jnp.where(mine, res[0:8] + res[16:24], 0.0))
                hi_feat.append(jnp.where(mine, res[8:16] + res[24:32], 0.0))
            return jnp.concatenate(lo_feat + hi_feat, axis=1)

        def commit(t0, parts):
            x2_ref[pl.ds(rows8(t0), 8), :] += (parts[0] + parts[1]) + (parts[2] + parts[3])

        _for_tokens_pipelined(tm, (ta, tb, tc, td), gather, compute, commit)

    _two_tile_step(idx_hbm, idx_smem, sem, process)


def _peer_v(idx_rows, w, x1, table, chunks):
    slots = w.shape[1]
    return _gather_call(_peer_v_kernel, "peer_v", idx_rows, table, chunks, [w, x1],
                        x1.shape[1], [pltpu.VMEM((4, 2 * GATHER_TOKENS, 2 * slots), F32)])


def _ple_kernel(x_ref, p_ref, gp_ref, wg_ref, wp_ref, gf_ref, o_ref, *, final_norm):
    x = x_ref[...]
    gate = jax.nn.sigmoid(_dot((_rms(x) * gp_ref[...]).astype(BF16), wg_ref[...]))
    x = x + gate * _dot(p_ref[...].astype(BF16), wp_ref[...])
    o_ref[...] = _rms(x) * gf_ref[...] if final_norm else x


def _ple(x2, p2, gp, wg, wp, gf, final_norm, tm=512):
    t, d = x2.shape
    pd = p2.shape[1]
    return pl.pallas_call(
        functools.partial(_ple_kernel, final_norm=final_norm),
        out_shape=jax.ShapeDtypeStruct((t, d), F32),
        grid=(t // tm,),
        in_specs=[pl.BlockSpec((tm, d), lambda i: (i, 0)),
                  pl.BlockSpec((tm, pd), lambda i: (i, 0)),
                  pl.BlockSpec((1, d), lambda i: (0, 0)),
                  pl.BlockSpec((d, d), lambda i: (0, 0)),
                  pl.BlockSpec((pd, d), lambda i: (0, 0)),
                  pl.BlockSpec((1, d), lambda i: (0, 0))],
        out_specs=pl.BlockSpec((tm, d), lambda i: (i, 0)),
        compiler_params=_cparams(1),
        name="ple",
    )(x2, p2, gp.reshape(1, d), wg, wp, gf.reshape(1, d))


def kernel(x, p, norm_mix, w_in, w_att_up, w_pool_group, pool_scale, w_out, norm_ffn,
           w_query, sub_keys, expert_u, expert_v, norm_ple, w_ple_gate, w_ple, norm_final):
    batch, seq, d = x.shape
    depth = w_in.shape[0]
    heads, head_dim = w_att_up.shape[1], w_att_up.shape[2]
    pool = w_pool_group.shape[1] * w_pool_group.shape[2]
    peer_heads = sub_keys.shape[1]
    xt = x.reshape(batch * seq, d)
    for i in range(depth):
        q, k, v, u, ga, gp = _in_proj(xt, norm_mix[i], w_in[i].astype(BF16),
                                      heads * head_dim, head_dim, pool)
        att = _attention(q, k, v, batch, seq, head_dim)
        x1, hn = _mix(xt, att, u, ga, gp, _pair_block_diag(w_att_up[i]).astype(BF16),
                      w_pool_group[i].astype(BF16), pool_scale[i], w_out[i].astype(BF16),
                      norm_ffn[i], batch, seq)
        idx_t, gate_t = _route(hn, w_query[i].astype(BF16), sub_keys[i].astype(BF16),
                               peer_heads)
        chunks = d // 256
        idx_rows = (idx_t * chunks).reshape(idx_t.shape[0], -1, GATHER_TOKENS).swapaxes(0, 1)
        gates = gate_t.T
        w = _peer_u(idx_rows, hn, gates, _pack_table(expert_u[i]), chunks)
        x2 = _peer_v(idx_rows, w, x1, _pack_table(expert_v[i]), chunks)
        xt = _ple(x2, p[i].reshape(batch * seq, -1), norm_ple[i],
                  w_ple_gate[i].astype(BF16), w_ple[i].astype(BF16), norm_final,
                  final_norm=(i == depth - 1))
    return xt.reshape(batch, seq, d)
```

```python
import functools
import math

import jax
import jax.numpy as jnp
from jax import lax
from jax.experimental import pallas as pl
from jax.experimental.pallas import tpu as pltpu

F32 = jnp.float32
BF16 = jnp.bfloat16

EPS = 1e-6
ATT_BLOCK = 128
SURVIVAL_UNDERFLOW = 150.0
POOL_WINDOWS = (2, 4, 8, 16)
PEER_TOPK = 16
N_KEYS = 128
V7X_VMEM_BYTES = 64 * 1024 * 1024
VMEM_LIMIT = V7X_VMEM_BYTES - 8 * 1024 * 1024
GATHER_TOKENS = 64
TRIP_TOKENS = 32
NEG_INF = float("-inf")
POS_SENTINEL = 1e9


def _cparams(n_axes):
    return pltpu.CompilerParams(
        dimension_semantics=("arbitrary",) * n_axes, vmem_limit_bytes=VMEM_LIMIT)


def _rms(x):
    return x * lax.rsqrt(jnp.mean(x * x, axis=-1, keepdims=True) + EPS)


def _dot(a, b):
    return jnp.dot(a, b, preferred_element_type=F32)


def _dot_nt(a, b):
    return lax.dot_general(a, b, (((1,), (1,)), ((), ())), preferred_element_type=F32)


def _split_bf16(x):
    hi = x.astype(BF16)
    lo = (x - hi.astype(F32)).astype(BF16)
    return hi, lo


def _in_proj_kernel(x_ref, g_ref, w_ref, q_ref, k_ref, v_ref, u_ref, ga_ref, gp_ref,
                    *, head_dim):
    h = (_rms(x_ref[...]) * g_ref[...]).astype(BF16)
    scale = head_dim ** -0.5
    lo = 0
    for ref, mul in ((q_ref, scale), (k_ref, None), (v_ref, None), (u_ref, None),
                     (ga_ref, None), (gp_ref, None)):
        width = ref.shape[1]
        y = _dot(h, w_ref[:, lo:lo + width])
        ref[...] = (y if mul is None else y * mul).astype(BF16)
        lo += width


def _in_proj(x2, g, w_bf, att, head_dim, pool, tm=512):
    t, d = x2.shape
    n = w_bf.shape[1]
    kern = functools.partial(_in_proj_kernel, head_dim=head_dim)
    widths = (att, att, att, pool, d, d)
    return pl.pallas_call(
        kern,
        out_shape=tuple(jax.ShapeDtypeStruct((t, w), BF16) for w in widths),
        grid=(t // tm,),
        in_specs=[pl.BlockSpec((tm, d), lambda i: (i, 0)),
                  pl.BlockSpec((1, d), lambda i: (0, 0)),
                  pl.BlockSpec((d, n), lambda i: (0, 0))],
        out_specs=tuple(pl.BlockSpec((tm, w), lambda i: (i, 0)) for w in widths),
        compiler_params=_cparams(1),
        name="in_proj",
    )(x2, g.reshape(1, d), w_bf)


def _attn_kernel(q_ref, k_ref, v_ref, o_ref, qm_ref, tri_ref, acc_ref, surv_ref,
                 *, head_dim):
    blk = ATT_BLOCK
    i = pl.program_id(1)
    pairs = q_ref.shape[1] // blk
    lane = lax.broadcasted_iota(jnp.int32, (blk, blk), 1)
    sub = lax.broadcasted_iota(jnp.int32, (blk, blk), 0)
    first = lane < head_dim
    for p in range(pairs):
        qp = q_ref[:, p * blk:(p + 1) * blk]
        zero = jnp.zeros_like(qp)
        qm_ref[p, 0:blk, :] = jnp.where(first, qp, zero)
        qm_ref[p, blk:2 * blk, :] = jnp.where(first, zero, qp)
    r2 = lax.broadcasted_iota(jnp.int32, (2 * blk, 2 * blk), 0) & (blk - 1)
    c2 = lax.broadcasted_iota(jnp.int32, (2 * blk, 2 * blk), 1)
    tri_ref[...] = jnp.where((r2 > c2) | (c2 >= blk), 1.0, 0.0).astype(BF16)
    acc_ref[...] = jnp.zeros_like(acc_ref)
    surv_ref[...] = jnp.zeros_like(surv_ref)
    causal = jnp.concatenate([lane < sub, lane < sub], axis=0)

    def block(j, mask):
        start = pl.multiple_of(j * blk, blk)
        log_beta, parts = [], []
        for p in range(pairs):
            kp = k_ref[pl.ds(start, blk), p * blk:(p + 1) * blk]
            z = _dot_nt(qm_ref[p], kp)
            sp = jnp.maximum(z, 0.0) + jnp.log(1.0 + jnp.exp(-jnp.abs(z)))
            log_beta.append(z - sp)
            if mask is not None:
                sp = jnp.where(mask, sp, 0.0)
            hi, lo = _split_bf16(sp)
            parts.append(jnp.concatenate([hi, lo], axis=1))
        sums = _dot(jnp.concatenate(parts, axis=0), tri_ref[...])
        for p in range(pairs):
            rows = slice(p * 2 * blk, (p + 1) * 2 * blk)
            later, total = sums[rows, :blk], sums[rows, blk:]
            w = jnp.exp(log_beta[p] - later - surv_ref[p])
            if mask is not None:
                w = jnp.where(mask, w, 0.0)
            vp = v_ref[pl.ds(start, blk), p * blk:(p + 1) * blk]
            acc_ref[p] += _dot(w.astype(BF16), vp)
            surv_ref[p] += total

    def any_alive():
        return jnp.min(surv_ref[...]) < SURVIVAL_UNDERFLOW

    block(i, causal)

    def cond(carry):
        return (carry[0] < (i >> 1)) & carry[1]

    def body(carry):
        n = carry[0]
        block(i - 1 - 2 * n, None)
        block(i - 2 - 2 * n, None)
        return n + 1, any_alive()

    _, alive = lax.while_loop(cond, body, (jnp.int32(0), any_alive()))

    @pl.when(((i & 1) == 1) & alive)
    def _():
        block(0, None)

    for p in range(pairs):
        o_ref[:, p * blk:(p + 1) * blk] = jnp.where(
            first, acc_ref[p, 0:blk, :], acc_ref[p, blk:2 * blk, :]).astype(o_ref.dtype)


def _attention(q, k, v, batch, seq, head_dim):
    t, width = q.shape
    blk = ATT_BLOCK
    assert 2 * head_dim == blk and width % blk == 0
    pairs = width // blk
    nq = seq // blk
    qspec = pl.BlockSpec((blk, width), lambda b, i: (b * nq + i, 0))
    kvspec = pl.BlockSpec((seq, width), lambda b, i: (b, 0))
    return pl.pallas_call(
        functools.partial(_attn_kernel, head_dim=head_dim),
        out_shape=jax.ShapeDtypeStruct((t, width), BF16),
        grid=(batch, nq),
        in_specs=[qspec, kvspec, kvspec],
        out_specs=qspec,
        scratch_shapes=[pltpu.VMEM((pairs, 2 * blk, blk), BF16),
                        pltpu.VMEM((2 * blk, 2 * blk), BF16),
                        pltpu.VMEM((pairs, 2 * blk, blk), F32),
                        pltpu.VMEM((pairs, 2 * blk, blk), F32)],
        compiler_params=_cparams(2),
        name="attn",
    )(q, k, v)


def _mix_kernel(x_ref, att_ref, u_ref, ga_ref, gp_ref, wa_ref, wp_ref, ps_ref, wo_ref,
                gn_ref, x1_ref, hn_ref, prev_ref, *, tm):
    s = pl.program_id(1)

    @pl.when(s == 0)
    def _():
        prev_ref[...] = jnp.zeros_like(prev_ref)

    pairs, pw, _ = wa_ref.shape
    a_up = jnp.concatenate(
        [_dot(att_ref[:, p * pw:(p + 1) * pw], wa_ref[p]) for p in range(pairs)],
        axis=1)

    u = u_ref[...]
    ext = jnp.concatenate([prev_ref[...], u], axis=0)
    prev_ref[...] = u
    r = lax.broadcasted_iota(jnp.int32, (tm, 2 * tm), 0) + tm
    c = lax.broadcasted_iota(jnp.int32, (tm, 2 * tm), 1)
    pos = (s * tm + lax.broadcasted_iota(jnp.int32, (tm, 1), 0) + 1).astype(F32)
    groups = len(POOL_WINDOWS)
    gdim = u.shape[1] // groups
    uf = u.astype(F32)
    p_parts = []
    for g, w in enumerate(POOL_WINDOWS):
        band = jnp.where((c <= r) & (c > r - w), 1.0, 0.0).astype(BF16)
        sl = slice(g * gdim, (g + 1) * gdim)
        window_sum = _dot(band, ext[:, sl])
        y = window_sum / jnp.minimum(pos, float(w)) - uf[:, sl]
        p_parts.append(_dot(y.astype(BF16), wp_ref[g]))
    p_up = jnp.concatenate(p_parts, axis=1) * ps_ref[...]

    merged = (jax.nn.sigmoid(ga_ref[...].astype(F32)) * a_up
              + jax.nn.sigmoid(gp_ref[...].astype(F32)) * p_up)
    x1 = x_ref[...] + _dot(merged.astype(BF16), wo_ref[...])
    x1_ref[...] = x1
    hn_ref[...] = (_rms(x1) * gn_ref[...]).astype(BF16)


def _pair_block_diag(w):
    h, a, b = w.shape
    z = jnp.zeros((h // 2, a, b), w.dtype)
    top = jnp.concatenate([w[0::2], z], axis=2)
    bottom = jnp.concatenate([z, w[1::2]], axis=2)
    return jnp.concatenate([top, bottom], axis=1)


def _mix(x2, att, u, ga, gp, wa, wp, ps, wo, gn, batch, seq, tm=256):
    t, d = x2.shape
    att_w = att.shape[1]
    pool = u.shape[1]
    ns = seq // tm
    tok = lambda b, s: (b * ns + s, 0)
    const2 = lambda b, s: (0, 0)
    const3 = lambda b, s: (0, 0, 0)
    kern = functools.partial(_mix_kernel, tm=tm)
    return pl.pallas_call(
        kern,
        out_shape=(jax.ShapeDtypeStruct((t, d), F32), jax.ShapeDtypeStruct((t, d), BF16)),
        grid=(batch, ns),
        in_specs=[pl.BlockSpec((tm, d), tok),
                  pl.BlockSpec((tm, att_w), tok),
                  pl.BlockSpec((tm, pool), tok),
                  pl.BlockSpec((tm, d), tok),
                  pl.BlockSpec((tm, d), tok),
                  pl.BlockSpec(wa.shape, const3),
                  pl.BlockSpec(wp.shape, const3),
                  pl.BlockSpec((1, d), const2),
                  pl.BlockSpec((d, d), const2),
                  pl.BlockSpec((1, d), const2)],
        out_specs=(pl.BlockSpec((tm, d), tok), pl.BlockSpec((tm, d), tok)),
        scratch_shapes=[pltpu.VMEM((tm, pool), BF16)],
        compiler_params=_cparams(2),
        name="mix",
    )(x2, att, u, ga, gp, wa, wp, ps.reshape(1, d), wo, gn.reshape(1, d))


def _topk_rows(s, payload, k, pos=None):
    if pos is None:
        pos = lax.broadcasted_iota(jnp.int32, s.shape, 0).astype(F32)
    vals, poss, pays = [], [], []
    for _ in range(k):
        m = jnp.max(s, axis=0, keepdims=True)
        p = jnp.min(jnp.where(s == m, pos, POS_SENTINEL), axis=0, keepdims=True)
        hit = pos == p
        if payload is not None:
            pays.append(jnp.sum(jnp.where(hit, payload, 0.0), axis=0, keepdims=True))
        s = jnp.where(hit, NEG_INF, s)
        vals.append(m)
        poss.append(p)
    cat = lambda xs: jnp.concatenate(xs, axis=0)
    return cat(vals), cat(poss), (cat(pays) if payload is not None else None)


def _route_kernel(hn_ref, wq_ref, keys_ref, idx_ref, gate_ref, *, heads):
    q = _dot(hn_ref[...], wq_ref[...]).astype(BF16)
    half = keys_ref.shape[-1]
    k = PEER_TOPK
    for hd in range(heads):
        tops = []
        for c in range(2):
            lo = (hd * 2 + c) * half
            scores = _dot_nt(keys_ref[hd, c], q[:, lo:lo + half])
            tops.append(_topk_rows(scores, None, k)[:2])
        (s0, i0), (s1, i1) = tops
        h = k // 2
        groups = ([(slice(0, 1), slice(0, k))]
                  + [(slice(a, a + 1), slice(0, h)) for a in range(1, h)]
                  + [(slice(h, k), slice(0, 1))])
        cand_s = jnp.concatenate([s0[ra, :] + s1[rb, :] for ra, rb in groups], axis=0)
        cand_i = jnp.concatenate(
            [i0[ra, :] * float(N_KEYS) + i1[rb, :] for ra, rb in groups], axis=0)
        n = s0.shape[1]
        row = lambda lo, cnt: (lax.broadcasted_iota(jnp.int32, (cnt, n), 0) + lo).astype(F32)
        cand_pos = jnp.concatenate(
            [row(0, k)] + [row(a * k, h) for a in range(1, h)] + [row(h, k - h) * float(k)],
            axis=0)
        best_s, _, best_i = _topk_rows(cand_s, cand_i, k, cand_pos)
        e = jnp.exp(best_s - best_s[0:1, :])
        gate_ref[hd * k:(hd + 1) * k, :] = e / jnp.sum(e, axis=0, keepdims=True)
        idx_ref[hd * k:(hd + 1) * k, :] = best_i.astype(jnp.int32)


def _route(hn, wq_bf, keys_bf, heads, tm=256):
    t, d = hn.shape
    nq = wq_bf.shape[1]
    slots = heads * PEER_TOPK
    kern = functools.partial(_route_kernel, heads=heads)
    return pl.pallas_call(
        kern,
        out_shape=(jax.ShapeDtypeStruct((slots, t), jnp.int32),
                   jax.ShapeDtypeStruct((slots, t), F32)),
        grid=(t // tm,),
        in_specs=[pl.BlockSpec((tm, d), lambda i: (i, 0)),
                  pl.BlockSpec((d, nq), lambda i: (0, 0)),
                  pl.BlockSpec(keys_bf.shape, lambda i: (0, 0, 0, 0))],
        out_specs=(pl.BlockSpec((slots, tm), lambda i: (0, i)),
                   pl.BlockSpec((slots, tm), lambda i: (0, i))),
        compiler_params=_cparams(1),
        name="route",
    )(hn, wq_bf, keys_bf)


def _pack_table(w):
    e, d = w.shape
    bits = lax.bitcast_convert_type(w.astype(BF16), jnp.uint16).astype(jnp.uint32)
    lo, hi = bits[:, :d // 2], bits[:, d // 2:]
    return (lo | (hi << 16)).reshape(e * (d // 256), 128)


def _gather_rows(idx_ref, tab_ref, tile_ref, t, slots, chunks):
    for k in range(slots):
        row = pl.multiple_of(idx_ref.at[k][t], chunks)
        tile_ref[k * chunks:(k + 1) * chunks, :] = tab_ref[pl.ds(row, chunks), :]


def _tile_chunk(tile_ref, c, slots):
    chunks = tile_ref.shape[0] // slots
    return pltpu.bitcast(tile_ref[pl.ds(c, slots, stride=chunks), :], BF16)


def _for_tokens_pipelined(tm, tiles, gather, compute, commit):
    a, b, c, d = tiles
    gather(0, a)
    gather(1, b)
    last = tm - 1

    def four_tokens(t0):
        ra = compute(t0, a)
        rb = compute(t0 + 1, b)
        gather(t0 + 2, c)
        gather(t0 + 3, d)
        rc = compute(t0 + 2, c)
        rd = compute(t0 + 3, d)
        gather(jnp.minimum(t0 + 4, last), a)
        gather(jnp.minimum(t0 + 5, last), b)
        commit(t0, (ra, rb, rc, rd))

    def trip(n, carry):
        for g in range(TRIP_TOKENS // 4):
            four_tokens(TRIP_TOKENS * n + 4 * g)
        return carry

    lax.fori_loop(0, tm // TRIP_TOKENS, trip, 0)


def _two_tile_step(idx_hbm, idx_smem, sem, process):
    step = pl.program_id(0)

    def index_copy(tile, slot):
        return pltpu.make_async_copy(idx_hbm.at[tile], idx_smem.at[slot], sem.at[slot])

    @pl.when(step == 0)
    def _():
        index_copy(0, 0).start()

    index_copy(2 * step, 0).wait()
    index_copy(2 * step + 1, 1).start()
    process(0, idx_smem.at[0])
    index_copy(2 * step + 1, 1).wait()

    @pl.when(step + 1 < pl.num_programs(0))
    def _():
        index_copy(2 * step + 2, 0).start()

    process(1, idx_smem.at[1])


def _gather_call(kernel_fn, name, idx_rows, table, chunks, blocked_inputs, out_width,
                 extra_scratch):
    n_tiles, slots, tm = idx_rows.shape
    t = n_tiles * tm
    tok = lambda width: pl.BlockSpec((2 * tm, width), lambda i: (i, 0))
    return pl.pallas_call(
        functools.partial(kernel_fn, chunks=chunks),
        out_shape=jax.ShapeDtypeStruct((t, out_width), F32),
        grid=(t // (2 * tm),),
        in_specs=([pl.BlockSpec(memory_space=pl.ANY)]
                  + [tok(x.shape[1]) for x in blocked_inputs]
                  + [pl.BlockSpec(table.shape, lambda i: (0, 0),
                                  pipeline_mode=pl.Buffered(1))]),
        out_specs=tok(out_width),
        scratch_shapes=[pltpu.VMEM((chunks * slots, 128), jnp.uint32)] * 4 + [
                        pltpu.SMEM((2, slots, tm), jnp.int32),
                        pltpu.SemaphoreType.DMA((2,))] + extra_scratch,
        compiler_params=_cparams(1),
        name=name,
    )(idx_rows, *blocked_inputs, table)


def _peer_u_kernel(idx_hbm, hn_ref, gate_ref, tab_ref, w_ref, ta, tb, tc, td, idx_smem, sem,
                   z_ref, *, chunks):
    tm = GATHER_TOKENS
    slots = gate_ref.shape[1]
    half = chunks * 128
    lane = lax.broadcasted_iota(jnp.int32, (2 * slots, 2 * tm), 1)

    def process(part, idx_ref):
        tok = slice(part * tm, (part + 1) * tm)
        hn = hn_ref[tok, :].astype(F32)
        rhs = jnp.concatenate([hn[:, :half], hn[:, half:]], axis=0).T.astype(BF16)
        z_ref[...] = jnp.zeros_like(z_ref)

        def gather(t, tile):
            _gather_rows(idx_ref, tab_ref, tile, t, slots, chunks)

        def compute(t, tile):
            rows = jnp.concatenate([_tile_chunk(tile, c, slots) for c in range(chunks)],
                                   axis=1)
            y = _dot(rows, rhs)
            return jnp.where((lane == t) | (lane == t + tm), y, 0.0)

        def commit(t0, parts):
            z_ref[...] += (parts[0] + parts[1]) + (parts[2] + parts[3])

        _for_tokens_pipelined(tm, (ta, tb, tc, td), gather, compute, commit)
        act_t = (z_ref[pl.ds(0, slots, stride=2), :][:, 0:tm]
                 + z_ref[pl.ds(1, slots, stride=2), :][:, tm:2 * tm])
        pre = act_t.T
        act = 0.5 * pre * (1.0 + lax.erf(pre * math.sqrt(0.5)))
        w_ref[tok, :] = gate_ref[tok, :] * act

    _two_tile_step(idx_hbm, idx_smem, sem, process)


def _peer_u(idx_rows, hn, gates, table, chunks):
    slots = gates.shape[1]
    return _gather_call(_peer_u_kernel, "peer_u", idx_rows, table, chunks, [hn, gates],
                        slots, [pltpu.VMEM((2 * slots, 2 * GATHER_TOKENS), F32)])


def _peer_v_kernel(idx_hbm, w_ref, x1_ref, tab_ref, x2_ref, ta, tb, tc, td, idx_smem, sem,
                   lhs_ref, *, chunks):
    tm = GATHER_TOKENS
    slots = w_ref.shape[1]
    half = chunks * 128
    r = lax.broadcasted_iota(jnp.int32, (slots, 2 * slots), 0)
    cc = lax.broadcasted_iota(jnp.int32, (slots, 2 * slots), 1)
    even = jnp.where(cc == 2 * r, 1.0, 0.0).astype(BF16)
    odd = jnp.where(cc == 2 * r + 1, 1.0, 0.0).astype(BF16)
    w_hi, w_lo = _split_bf16(w_ref[...])
    lhs_ref[0] = _dot(w_hi, even)
    lhs_ref[1] = _dot(w_hi, odd)
    lhs_ref[2] = _dot(w_lo, even)
    lhs_ref[3] = _dot(w_lo, odd)
    x2_ref[...] = x1_ref[...]
    sub = lax.broadcasted_iota(jnp.int32, (8, 128), 0)

    def process(part, idx_ref):
        def gather(t, tile):
            _gather_rows(idx_ref, tab_ref, tile, t, slots, chunks)

        def rows8(t):
            return pl.multiple_of(part * tm + ((t >> 3) << 3), 8)

        def compute(t, tile):
            g8 = rows8(t)
            lhs = jnp.concatenate([lhs_ref[v, pl.ds(g8, 8), :] for v in range(4)],
                                  axis=0).astype(BF16)
            mine = sub == (t & 7)
            lo_feat, hi_feat = [], []
            for c in range(chunks):
                res = _dot(lhs, _tile_chunk(tile, c, slots))
                lo_feat.append(jnp.where(mine, res[0:8] + res[16:24], 0.0))
                hi_feat.append(jnp.where(mine, res[8:16] + res[24:32], 0.0))
            return jnp.concatenate(lo_feat + hi_feat, axis=1)

        def commit(t0, parts):
            x2_ref[pl.ds(rows8(t0), 8), :] += (parts[0] + parts[1]) + (parts[2] + parts[3])

        _for_tokens_pipelined(tm, (ta, tb, tc, td), gather, compute, commit)

    _two_tile_step(idx_hbm, idx_smem, sem, process)


def _peer_v(idx_rows, w, x1, table, chunks):
    slots = w.shape[1]
    return _gather_call(_peer_v_kernel, "peer_v", idx_rows, table, chunks, [w, x1],
                        x1.shape[1], [pltpu.VMEM((4, 2 * GATHER_TOKENS, 2 * slots), F32)])


def _ple_kernel(x_ref, p_ref, gp_ref, wg_ref, wp_ref, gf_ref, o_ref, *, final_norm):
    x = x_ref[...]
    gate = jax.nn.sigmoid(_dot((_rms(x) * gp_ref[...]).astype(BF16), wg_ref[...]))
    x = x + gate * _dot(p_ref[...].astype(BF16), wp_ref[...])
    o_ref[...] = _rms(x) * gf_ref[...] if final_norm else x


def _ple(x2, p2, gp, wg, wp, gf, final_norm, tm=512):
    t, d = x2.shape
    pd = p2.shape[1]
    return pl.pallas_call(
        functools.partial(_ple_kernel, final_norm=final_norm),
        out_shape=jax.ShapeDtypeStruct((t, d), F32),
        grid=(t // tm,),
        in_specs=[pl.BlockSpec((tm, d), lambda i: (i, 0)),
                  pl.BlockSpec((tm, pd), lambda i: (i, 0)),
                  pl.BlockSpec((1, d), lambda i: (0, 0)),
                  pl.BlockSpec((d, d), lambda i: (0, 0)),
                  pl.BlockSpec((pd, d), lambda i: (0, 0)),
                  pl.BlockSpec((1, d), lambda i: (0, 0))],
        out_specs=pl.BlockSpec((tm, d), lambda i: (i, 0)),
        compiler_params=_cparams(1),
        name="ple",
    )(x2, p2, gp.reshape(1, d), wg, wp, gf.reshape(1, d))


def kernel(x, p, norm_mix, w_in, w_att_up, w_pool_group, pool_scale, w_out, norm_ffn,
           w_query, sub_keys, expert_u, expert_v, norm_ple, w_ple_gate, w_ple, norm_final):
    batch, seq, d = x.shape
    depth = w_in.shape[0]
    heads, head_dim = w_att_up.shape[1], w_att_up.shape[2]
    pool = w_pool_group.shape[1] * w_pool_group.shape[2]
    peer_heads = sub_keys.shape[1]
    xt = x.reshape(batch * seq, d)
    for i in range(depth):
        q, k, v, u, ga, gp = _in_proj(xt, norm_mix[i], w_in[i].astype(BF16),
                                      heads * head_dim, head_dim, pool)
        att = _attention(q, k, v, batch, seq, head_dim)
        x1, hn = _mix(xt, att, u, ga, gp, _pair_block_diag(w_att_up[i]).astype(BF16),
                      w_pool_group[i].astype(BF16), pool_scale[i], w_out[i].astype(BF16),
                      norm_ffn[i], batch, seq)
        idx_t, gate_t = _route(hn, w_query[i].astype(BF16), sub_keys[i].astype(BF16),
                               peer_heads)
        chunks = d // 256
        idx_rows = (idx_t * chunks).reshape(idx_t.shape[0], -1, GATHER_TOKENS).swapaxes(0, 1)
        gates = gate_t.T
        w = _peer_u(idx_rows, hn, gates, _pack_table(expert_u[i]), chunks)
        x2 = _peer_v(idx_rows, w, x1, _pack_table(expert_v[i]), chunks)
        xt = _ple(x2, p[i].reshape(batch * seq, -1), norm_ple[i],
                  w_ple_gate[i].astype(BF16), w_ple[i].astype(BF16), norm_final,
                  final_norm=(i == depth - 1))
    return xt.reshape(batch, seq, d)
```

```python
import functools
import math

import jax
import jax.numpy as jnp
from jax import lax
from jax.experimental import pallas as pl
from jax.experimental.pallas import tpu as pltpu

F32 = jnp.float32
BF16 = jnp.bfloat16

EPS = 1e-6
ATT_BLOCK = 128
SURVIVAL_UNDERFLOW = 150.0
POOL_WINDOWS = (2, 4, 8, 16)
PEER_TOPK = 16
N_KEYS = 128
V7X_VMEM_BYTES = 64 * 1024 * 1024
VMEM_LIMIT = V7X_VMEM_BYTES - 8 * 1024 * 1024
GATHER_TOKENS = 64
TRIP_TOKENS = 64
NEG_INF = float("-inf")
POS_SENTINEL = 1e9


def _cparams(n_axes):
    return pltpu.CompilerParams(
        dimension_semantics=("arbitrary",) * n_axes, vmem_limit_bytes=VMEM_LIMIT)


def _rms(x):
    return x * lax.rsqrt(jnp.mean(x * x, axis=-1, keepdims=True) + EPS)


def _dot(a, b):
    return jnp.dot(a, b, preferred_element_type=F32)


def _dot_nt(a, b):
    return lax.dot_general(a, b, (((1,), (1,)), ((), ())), preferred_element_type=F32)


def _split_bf16(x):
    hi = x.astype(BF16)
    lo = (x - hi.astype(F32)).astype(BF16)
    return hi, lo


def _in_proj_kernel(x_ref, g_ref, w_ref, q_ref, k_ref, v_ref, u_ref, ga_ref, gp_ref,
                    *, head_dim):
    h = (_rms(x_ref[...]) * g_ref[...]).astype(BF16)
    scale = head_dim ** -0.5
    lo = 0
    for ref, mul in ((q_ref, scale), (k_ref, None), (v_ref, None), (u_ref, None),
                     (ga_ref, None), (gp_ref, None)):
        width = ref.shape[1]
        y = _dot(h, w_ref[:, lo:lo + width])
        ref[...] = (y if mul is None else y * mul).astype(BF16)
        lo += width


def _in_proj(x2, g, w_bf, att, head_dim, pool, tm=512):
    t, d = x2.shape
    n = w_bf.shape[1]
    kern = functools.partial(_in_proj_kernel, head_dim=head_dim)
    widths = (att, att, att, pool, d, d)
    return pl.pallas_call(
        kern,
        out_shape=tuple(jax.ShapeDtypeStruct((t, w), BF16) for w in widths),
        grid=(t // tm,),
        in_specs=[pl.BlockSpec((tm, d), lambda i: (i, 0)),
                  pl.BlockSpec((1, d), lambda i: (0, 0)),
                  pl.BlockSpec((d, n), lambda i: (0, 0))],
        out_specs=tuple(pl.BlockSpec((tm, w), lambda i: (i, 0)) for w in widths),
        compiler_params=_cparams(1),
        name="in_proj",
    )(x2, g.reshape(1, d), w_bf)


def _attn_kernel(q_ref, k_ref, v_ref, o_ref, qm_ref, tri_ref, acc_ref, surv_ref,
                 *, head_dim):
    blk = ATT_BLOCK
    i = pl.program_id(1)
    pairs = q_ref.shape[1] // blk
    lane = lax.broadcasted_iota(jnp.int32, (blk, blk), 1)
    sub = lax.broadcasted_iota(jnp.int32, (blk, blk), 0)
    first = lane < head_dim
    for p in range(pairs):
        qp = q_ref[:, p * blk:(p + 1) * blk]
        zero = jnp.zeros_like(qp)
        qm_ref[p, 0:blk, :] = jnp.where(first, qp, zero)
        qm_ref[p, blk:2 * blk, :] = jnp.where(first, zero, qp)
    r2 = lax.broadcasted_iota(jnp.int32, (2 * blk, 2 * blk), 0) & (blk - 1)
    c2 = lax.broadcasted_iota(jnp.int32, (2 * blk, 2 * blk), 1)
    tri_ref[...] = jnp.where((r2 > c2) | (c2 >= blk), 1.0, 0.0).astype(BF16)
    acc_ref[...] = jnp.zeros_like(acc_ref)
    surv_ref[...] = jnp.zeros_like(surv_ref)
    causal = jnp.concatenate([lane < sub, lane < sub], axis=0)

    def block(j, mask):
        start = pl.multiple_of(j * blk, blk)
        log_beta, parts = [], []
        for p in range(pairs):
            kp = k_ref[pl.ds(start, blk), p * blk:(p + 1) * blk]
            z = _dot_nt(qm_ref[p], kp)
            sp = jnp.maximum(z, 0.0) + jnp.log(1.0 + jnp.exp(-jnp.abs(z)))
            log_beta.append(z - sp)
            if mask is not None:
                sp = jnp.where(mask, sp, 0.0)
            hi, lo = _split_bf16(sp)
            parts.append(jnp.concatenate([hi, lo], axis=1))
        sums = _dot(jnp.concatenate(parts, axis=0), tri_ref[...])
        for p in range(pairs):
            rows = slice(p * 2 * blk, (p + 1) * 2 * blk)
            later, total = sums[rows, :blk], sums[rows, blk:]
            w = jnp.exp(log_beta[p] - later - surv_ref[p])
            if mask is not None:
                w = jnp.where(mask, w, 0.0)
            vp = v_ref[pl.ds(start, blk), p * blk:(p + 1) * blk]
            acc_ref[p] += _dot(w.astype(BF16), vp)
            surv_ref[p] += total

    def any_alive():
        return jnp.min(surv_ref[...]) < SURVIVAL_UNDERFLOW

    block(i, causal)

    def cond(carry):
        return (carry[0] < (i >> 1)) & carry[1]

    def body(carry):
        n = carry[0]
        block(i - 1 - 2 * n, None)
        block(i - 2 - 2 * n, None)
        return n + 1, any_alive()

    _, alive = lax.while_loop(cond, body, (jnp.int32(0), any_alive()))

    @pl.when(((i & 1) == 1) & alive)
    def _():
        block(0, None)

    for p in range(pairs):
        o_ref[:, p * blk:(p + 1) * blk] = jnp.where(
            first, acc_ref[p, 0:blk, :], acc_ref[p, blk:2 * blk, :]).astype(o_ref.dtype)


def _attention(q, k, v, batch, seq, head_dim):
    t, width = q.shape
    blk = ATT_BLOCK
    assert 2 * head_dim == blk and width % blk == 0
    pairs = width // blk
    nq = seq // blk
    qspec = pl.BlockSpec((blk, width), lambda b, i: (b * nq + i, 0))
    kvspec = pl.BlockSpec((seq, width), lambda b, i: (b, 0))
    return pl.pallas_call(
        functools.partial(_attn_kernel, head_dim=head_dim),
        out_shape=jax.ShapeDtypeStruct((t, width), BF16),
        grid=(batch, nq),
        in_specs=[qspec, kvspec, kvspec],
        out_specs=qspec,
        scratch_shapes=[pltpu.VMEM((pairs, 2 * blk, blk), BF16),
                        pltpu.VMEM((2 * blk, 2 * blk), BF16),
                        pltpu.VMEM((pairs, 2 * blk, blk), F32),
                        pltpu.VMEM((pairs, 2 * blk, blk), F32)],
        compiler_params=_cparams(2),
        name="attn",
    )(q, k, v)


def _mix_kernel(x_ref, att_ref, u_ref, ga_ref, gp_ref, wa_ref, wp_ref, ps_ref, wo_ref,
                gn_ref, x1_ref, hn_ref, prev_ref, *, tm):
    s = pl.program_id(1)

    @pl.when(s == 0)
    def _():
        prev_ref[...] = jnp.zeros_like(prev_ref)

    pairs, pw, _ = wa_ref.shape
    a_up = jnp.concatenate(
        [_dot(att_ref[:, p * pw:(p + 1) * pw], wa_ref[p]) for p in range(pairs)],
        axis=1)

    u = u_ref[...]
    ext = jnp.concatenate([prev_ref[...], u], axis=0)
    prev_ref[...] = u
    r = lax.broadcasted_iota(jnp.int32, (tm, 2 * tm), 0) + tm
    c = lax.broadcasted_iota(jnp.int32, (tm, 2 * tm), 1)
    pos = (s * tm + lax.broadcasted_iota(jnp.int32, (tm, 1), 0) + 1).astype(F32)
    groups = len(POOL_WINDOWS)
    gdim = u.shape[1] // groups
    uf = u.astype(F32)
    p_parts = []
    for g, w in enumerate(POOL_WINDOWS):
        band = jnp.where((c <= r) & (c > r - w), 1.0, 0.0).astype(BF16)
        sl = slice(g * gdim, (g + 1) * gdim)
        window_sum = _dot(band, ext[:, sl])
        y = window_sum / jnp.minimum(pos, float(w)) - uf[:, sl]
        p_parts.append(_dot(y.astype(BF16), wp_ref[g]))
    p_up = jnp.concatenate(p_parts, axis=1) * ps_ref[...]

    merged = (jax.nn.sigmoid(ga_ref[...].astype(F32)) * a_up
              + jax.nn.sigmoid(gp_ref[...].astype(F32)) * p_up)
    x1 = x_ref[...] + _dot(merged.astype(BF16), wo_ref[...])
    x1_ref[...] = x1
    hn_ref[...] = (_rms(x1) * gn_ref[...]).astype(BF16)


def _pair_block_diag(w):
    h, a, b = w.shape
    z = jnp.zeros((h // 2, a, b), w.dtype)
    top = jnp.concatenate([w[0::2], z], axis=2)
    bottom = jnp.concatenate([z, w[1::2]], axis=2)
    return jnp.concatenate([top, bottom], axis=1)


def _mix(x2, att, u, ga, gp, wa, wp, ps, wo, gn, batch, seq, tm=256):
    t, d = x2.shape
    att_w = att.shape[1]
    pool = u.shape[1]
    ns = seq // tm
    tok = lambda b, s: (b * ns + s, 0)
    const2 = lambda b, s: (0, 0)
    const3 = lambda b, s: (0, 0, 0)
    kern = functools.partial(_mix_kernel, tm=tm)
    return pl.pallas_call(
        kern,
        out_shape=(jax.ShapeDtypeStruct((t, d), F32), jax.ShapeDtypeStruct((t, d), BF16)),
        grid=(batch, ns),
        in_specs=[pl.BlockSpec((tm, d), tok),
                  pl.BlockSpec((tm, att_w), tok),
                  pl.BlockSpec((tm, pool), tok),
                  pl.BlockSpec((tm, d), tok),
                  pl.BlockSpec((tm, d), tok),
                  pl.BlockSpec(wa.shape, const3),
                  pl.BlockSpec(wp.shape, const3),
                  pl.BlockSpec((1, d), const2),
                  pl.BlockSpec((d, d), const2),
                  pl.BlockSpec((1, d), const2)],
        out_specs=(pl.BlockSpec((tm, d), tok), pl.BlockSpec((tm, d), tok)),
        scratch_shapes=[pltpu.VMEM((tm, pool), BF16)],
        compiler_params=_cparams(2),
        name="mix",
    )(x2, att, u, ga, gp, wa, wp, ps.reshape(1, d), wo, gn.reshape(1, d))


def _topk_rows(s, payload, k, pos=None):
    if pos is None:
        pos = lax.broadcasted_iota(jnp.int32, s.shape, 0).astype(F32)
    vals, poss, pays = [], [], []
    for _ in range(k):
        m = jnp.max(s, axis=0, keepdims=True)
        p = jnp.min(jnp.where(s == m, pos, POS_SENTINEL), axis=0, keepdims=True)
        hit = pos == p
        if payload is not None:
            pays.append(jnp.sum(jnp.where(hit, payload, 0.0), axis=0, keepdims=True))
        s = jnp.where(hit, NEG_INF, s)
        vals.append(m)
        poss.append(p)
    cat = lambda xs: jnp.concatenate(xs, axis=0)
    return cat(vals), cat(poss), (cat(pays) if payload is not None else None)


def _route_kernel(hn_ref, wq_ref, keys_ref, idx_ref, gate_ref, *, heads):
    q = _dot(hn_ref[...], wq_ref[...]).astype(BF16)
    half = keys_ref.shape[-1]
    k = PEER_TOPK
    for hd in range(heads):
        tops = []
        for c in range(2):
            lo = (hd * 2 + c) * half
            scores = _dot_nt(keys_ref[hd, c], q[:, lo:lo + half])
            tops.append(_topk_rows(scores, None, k)[:2])
        (s0, i0), (s1, i1) = tops
        h = k // 2
        groups = ([(slice(0, 1), slice(0, k))]
                  + [(slice(a, a + 1), slice(0, h)) for a in range(1, h)]
                  + [(slice(h, k), slice(0, 1))])
        cand_s = jnp.concatenate([s0[ra, :] + s1[rb, :] for ra, rb in groups], axis=0)
        cand_i = jnp.concatenate(
            [i0[ra, :] * float(N_KEYS) + i1[rb, :] for ra, rb in groups], axis=0)
        n = s0.shape[1]
        row = lambda lo, cnt: (lax.broadcasted_iota(jnp.int32, (cnt, n), 0) + lo).astype(F32)
        cand_pos = jnp.concatenate(
            [row(0, k)] + [row(a * k, h) for a in range(1, h)] + [row(h, k - h) * float(k)],
            axis=0)
        best_s, _, best_i = _topk_rows(cand_s, cand_i, k, cand_pos)
        e = jnp.exp(best_s - best_s[0:1, :])
        gate_ref[hd * k:(hd + 1) * k, :] = e / jnp.sum(e, axis=0, keepdims=True)
        idx_ref[hd * k:(hd + 1) * k, :] = best_i.astype(jnp.int32)


def _route(hn, wq_bf, keys_bf, heads, tm=256):
    t, d = hn.shape
    nq = wq_bf.shape[1]
    slots = heads * PEER_TOPK
    kern = functools.partial(_route_kernel, heads=heads)
    return pl.pallas_call(
        kern,
        out_shape=(jax.ShapeDtypeStruct((slots, t), jnp.int32),
                   jax.ShapeDtypeStruct((slots, t), F32)),
        grid=(t // tm,),
        in_specs=[pl.BlockSpec((tm, d), lambda i: (i, 0)),
                  pl.BlockSpec((d, nq), lambda i: (0, 0)),
                  pl.BlockSpec(keys_bf.shape, lambda i: (0, 0, 0, 0))],
        out_specs=(pl.BlockSpec((slots, tm), lambda i: (0, i)),
                   pl.BlockSpec((slots, tm), lambda i: (0, i))),
        compiler_params=_cparams(1),
        name="route",
    )(hn, wq_bf, keys_bf)


def _pack_table(w):
    e, d = w.shape
    bits = lax.bitcast_convert_type(w.astype(BF16), jnp.uint16).astype(jnp.uint32)
    lo, hi = bits[:, :d // 2], bits[:, d // 2:]
    return (lo | (hi << 16)).reshape(e * (d // 256), 128)


def _gather_rows(idx_ref, tab_ref, tile_ref, t, slots, chunks):
    for k in range(slots):
        row = pl.multiple_of(idx_ref.at[k][t], chunks)
        tile_ref[k * chunks:(k + 1) * chunks, :] = tab_ref[pl.ds(row, chunks), :]


def _tile_chunk(tile_ref, c, slots):
    chunks = tile_ref.shape[0] // slots
    return pltpu.bitcast(tile_ref[pl.ds(c, slots, stride=chunks), :], BF16)


def _for_tokens_pipelined(tm, tiles, gather, compute, commit):
    a, b, c, d = tiles
    gather(0, a)
    gather(1, b)
    last = tm - 1

    def four_tokens(t0):
        ra = compute(t0, a)
        rb = compute(t0 + 1, b)
        gather(t0 + 2, c)
        gather(t0 + 3, d)
        rc = compute(t0 + 2, c)
        rd = compute(t0 + 3, d)
        gather(jnp.minimum(t0 + 4, last), a)
        gather(jnp.minimum(t0 + 5, last), b)
        commit(t0, (ra, rb, rc, rd))

    def trip(n, carry):
        for g in range(TRIP_TOKENS // 4):
            four_tokens(TRIP_TOKENS * n + 4 * g)
        return carry

    lax.fori_loop(0, tm // TRIP_TOKENS, trip, 0)


def _two_tile_step(idx_hbm, idx_smem, sem, process):
    step = pl.program_id(0)

    def index_copy(tile, slot):
        return pltpu.make_async_copy(idx_hbm.at[tile], idx_smem.at[slot], sem.at[slot])

    @pl.when(step == 0)
    def _():
        index_copy(0, 0).start()

    index_copy(2 * step, 0).wait()
    index_copy(2 * step + 1, 1).start()
    process(0, idx_smem.at[0])
    index_copy(2 * step + 1, 1).wait()

    @pl.when(step + 1 < pl.num_programs(0))
    def _():
        index_copy(2 * step + 2, 0).start()

    process(1, idx_smem.at[1])


def _gather_call(kernel_fn, name, idx_rows, table, chunks, blocked_inputs, out_width,
                 extra_scratch):
    n_tiles, slots, tm = idx_rows.shape
    t = n_tiles * tm
    tok = lambda width: pl.BlockSpec((2 * tm, width), lambda i: (i, 0))
    return pl.pallas_call(
        functools.partial(kernel_fn, chunks=chunks),
        out_shape=jax.ShapeDtypeStruct((t, out_width), F32),
        grid=(t // (2 * tm),),
        in_specs=([pl.BlockSpec(memory_space=pl.ANY)]
                  + [tok(x.shape[1]) for x in blocked_inputs]
                  + [pl.BlockSpec(table.shape, lambda i: (0, 0),
                                  pipeline_mode=pl.Buffered(1))]),
        out_specs=tok(out_width),
        scratch_shapes=[pltpu.VMEM((chunks * slots, 128), jnp.uint32)] * 4 + [
                        pltpu.SMEM((2, slots, tm), jnp.int32),
                        pltpu.SemaphoreType.DMA((2,))] + extra_scratch,
        compiler_params=_cparams(1),
        name=name,
    )(idx_rows, *blocked_inputs, table)


def _peer_u_kernel(idx_hbm, hn_ref, gate_ref, tab_ref, w_ref, ta, tb, tc, td, idx_smem, sem,
                   z_ref, *, chunks):
    tm = GATHER_TOKENS
    slots = gate_ref.shape[1]
    half = chunks * 128
    lane = lax.broadcasted_iota(jnp.int32, (2 * slots, 2 * tm), 1)

    def process(part, idx_ref):
        tok = slice(part * tm, (part + 1) * tm)
        hn = hn_ref[tok, :].astype(F32)
        rhs = jnp.concatenate([hn[:, :half], hn[:, half:]], axis=0).T.astype(BF16)
        z_ref[...] = jnp.zeros_like(z_ref)

        def gather(t, tile):
            _gather_rows(idx_ref, tab_ref, tile, t, slots, chunks)

        def compute(t, tile):
            rows = jnp.concatenate([_tile_chunk(tile, c, slots) for c in range(chunks)],
                                   axis=1)
            y = _dot(rows, rhs)
            return jnp.where((lane == t) | (lane == t + tm), y, 0.0)

        def commit(t0, parts):
            z_ref[...] += (parts[0] + parts[1]) + (parts[2] + parts[3])

        _for_tokens_pipelined(tm, (ta, tb, tc, td), gather, compute, commit)
        act_t = (z_ref[pl.ds(0, slots, stride=2), :][:, 0:tm]
                 + z_ref[pl.ds(1, slots, stride=2), :][:, tm:2 * tm])
        pre = act_t.T
        act = 0.5 * pre * (1.0 + lax.erf(pre * math.sqrt(0.5)))
        w_ref[tok, :] = gate_ref[tok, :] * act

    _two_tile_step(idx_hbm, idx_smem, sem, process)


def _peer_u(idx_rows, hn, gates, table, chunks):
    slots = gates.shape[1]
    return _gather_call(_peer_u_kernel, "peer_u", idx_rows, table, chunks, [hn, gates],
                        slots, [pltpu.VMEM((2 * slots, 2 * GATHER_TOKENS), F32)])


def _peer_v_kernel(idx_hbm, w_ref, x1_ref, tab_ref, x2_ref, ta, tb, tc, td, idx_smem, sem,
                   lhs_ref, *, chunks):
    tm = GATHER_TOKENS
    slots = w_ref.shape[1]
    half = chunks * 128
    r = lax.broadcasted_iota(jnp.int32, (slots, 2 * slots), 0)
    cc = lax.broadcasted_iota(jnp.int32, (slots, 2 * slots), 1)
    even = jnp.where(cc == 2 * r, 1.0, 0.0).astype(BF16)
    odd = jnp.where(cc == 2 * r + 1, 1.0, 0.0).astype(BF16)
    w_hi, w_lo = _split_bf16(w_ref[...])
    lhs_ref[0] = _dot(w_hi, even)
    lhs_ref[1] = _dot(w_hi, odd)
    lhs_ref[2] = _dot(w_lo, even)
    lhs_ref[3] = _dot(w_lo, odd)
    x2_ref[...] = x1_ref[...]
    sub = lax.broadcasted_iota(jnp.int32, (8, 128), 0)

    def process(part, idx_ref):
        def gather(t, tile):
            _gather_rows(idx_ref, tab_ref, tile, t, slots, chunks)

        def rows8(t):
            return pl.multiple_of(part * tm + ((t >> 3) << 3), 8)

        def compute(t, tile):
            g8 = rows8(t)
            lhs = jnp.concatenate([lhs_ref[v, pl.ds(g8, 8), :] for v in range(4)],
                                  axis=0).astype(BF16)
            mine = sub == (t & 7)
            lo_feat, hi_feat = [], []
            for c in range(chunks):
                res = _dot(lhs, _tile_chunk(tile, c, slots))
                lo_feat.append(jnp.where(mine, res[0:8] + res[16:24], 0.0))
                hi_feat.append(jnp.where(mine, res[8:16] + res[24:32], 0.0))
            return jnp.concatenate(lo_feat + hi_feat, axis=1)

        def commit(t0, parts):
            x2_ref[pl.ds(rows8(t0), 8), :] += (parts[0] + parts[1]) + (parts[2] + parts[3])

        _for_tokens_pipelined(tm, (ta, tb, tc, td), gather, compute, commit)

    _two_tile_step(idx_hbm, idx_smem, sem, process)


def _peer_v(idx_rows, w, x1, table, chunks):
    slots = w.shape[1]
    return _gather_call(_peer_v_kernel, "peer_v", idx_rows, table, chunks, [w, x1],
                        x1.shape[1], [pltpu.VMEM((4, 2 * GATHER_TOKENS, 2 * slots), F32)])


def _ple_kernel(x_ref, p_ref, gp_ref, wg_ref, wp_ref, gf_ref, o_ref, *, final_norm):
    x = x_ref[...]
    gate = jax.nn.sigmoid(_dot((_rms(x) * gp_ref[...]).astype(BF16), wg_ref[...]))
    x = x + gate * _dot(p_ref[...].astype(BF16), wp_ref[...])
    o_ref[...] = _rms(x) * gf_ref[...] if final_norm else x


def _ple(x2, p2, gp, wg, wp, gf, final_norm, tm=512):
    t, d = x2.shape
    pd = p2.shape[1]
    return pl.pallas_call(
        functools.partial(_ple_kernel, final_norm=final_norm),
        out_shape=jax.ShapeDtypeStruct((t, d), F32),
        grid=(t // tm,),
        in_specs=[pl.BlockSpec((tm, d), lambda i: (i, 0)),
                  pl.BlockSpec((tm, pd), lambda i: (i, 0)),
                  pl.BlockSpec((1, d), lambda i: (0, 0)),
                  pl.BlockSpec((d, d), lambda i: (0, 0)),
                  pl.BlockSpec((pd, d), lambda i: (0, 0)),
                  pl.BlockSpec((1, d), lambda i: (0, 0))],
        out_specs=pl.BlockSpec((tm, d), lambda i: (i, 0)),
        compiler_params=_cparams(1),
        name="ple",
    )(x2, p2, gp.reshape(1, d), wg, wp, gf.reshape(1, d))


def kernel(x, p, norm_mix, w_in, w_att_up, w_pool_group, pool_scale, w_out, norm_ffn,
           w_query, sub_keys, expert_u, expert_v, norm_ple, w_ple_gate, w_ple, norm_final):
    batch, seq, d = x.shape
    depth = w_in.shape[0]
    heads, head_dim = w_att_up.shape[1], w_att_up.shape[2]
    pool = w_pool_group.shape[1] * w_pool_group.shape[2]
    peer_heads = sub_keys.shape[1]
    xt = x.reshape(batch * seq, d)
    for i in range(depth):
        q, k, v, u, ga, gp = _in_proj(xt, norm_mix[i], w_in[i].astype(BF16),
                                      heads * head_dim, head_dim, pool)
        att = _attention(q, k, v, batch, seq, head_dim)
        x1, hn = _mix(xt, att, u, ga, gp, _pair_block_diag(w_att_up[i]).astype(BF16),
                      w_pool_group[i].astype(BF16), pool_scale[i], w_out[i].astype(BF16),
                      norm_ffn[i], batch, seq)
        idx_t, gate_t = _route(hn, w_query[i].astype(BF16), sub_keys[i].astype(BF16),
                               peer_heads)
        chunks = d // 256
        idx_rows = (idx_t * chunks).reshape(idx_t.shape[0], -1, GATHER_TOKENS).swapaxes(0, 1)
        gates = gate_t.T
        w = _peer_u(idx_rows, hn, gates, _pack_table(expert_u[i]), chunks)
        x2 = _peer_v(idx_rows, w, x1, _pack_table(expert_v[i]), chunks)
        xt = _ple(x2, p[i].reshape(batch * seq, -1), norm_ple[i],
                  w_ple_gate[i].astype(BF16), w_ple[i].astype(BF16), norm_final,
                  final_norm=(i == depth - 1))
    return xt.reshape(batch, seq, d)
```

```python
import functools
import math

import jax
import jax.numpy as jnp
from jax import lax
from jax.experimental import pallas as pl
from jax.experimental.pallas import tpu as pltpu

F32 = jnp.float32
BF16 = jnp.bfloat16

EPS = 1e-6
ATT_BLOCK = 128
SURVIVAL_UNDERFLOW = 150.0
POOL_WINDOWS = (2, 4, 8, 16)
PEER_TOPK = 16
N_KEYS = 128
V7X_VMEM_BYTES = 64 * 1024 * 1024
LANES = 128
VMEM_LIMIT = V7X_VMEM_BYTES - 8 * 1024 * 1024
GATHER_TOKENS = 64
NEG_INF = float("-inf")
POS_SENTINEL = 1e9


def _cparams(n_axes):
    return pltpu.CompilerParams(
        dimension_semantics=("arbitrary",) * n_axes, vmem_limit_bytes=VMEM_LIMIT)


def _rms(x):
    return x * lax.rsqrt(jnp.mean(x * x, axis=-1, keepdims=True) + EPS)


def _dot(a, b):
    return jnp.dot(a, b, preferred_element_type=F32)


def _dot_nt(a, b):
    return lax.dot_general(a, b, (((1,), (1,)), ((), ())), preferred_element_type=F32)


def _split_bf16(x):
    hi = x.astype(BF16)
    lo = (x - hi.astype(F32)).astype(BF16)
    return hi, lo


def _in_proj_kernel(x_ref, g_ref, w_ref, q_ref, k_ref, v_ref, u_ref, ga_ref, gp_ref,
                    *, head_dim):
    h = (_rms(x_ref[...]) * g_ref[...]).astype(BF16)
    scale = head_dim ** -0.5
    lo = 0
    for ref, mul in ((q_ref, scale), (k_ref, None), (v_ref, None), (u_ref, None),
                     (ga_ref, None), (gp_ref, None)):
        width = ref.shape[1]
        y = _dot(h, w_ref[:, lo:lo + width])
        ref[...] = (y if mul is None else y * mul).astype(BF16)
        lo += width


def _in_proj(x2, g, w_bf, att, head_dim, pool, tm=512):
    t, d = x2.shape
    n = w_bf.shape[1]
    kern = functools.partial(_in_proj_kernel, head_dim=head_dim)
    widths = (att, att, att, pool, d, d)
    return pl.pallas_call(
        kern,
        out_shape=tuple(jax.ShapeDtypeStruct((t, w), BF16) for w in widths),
        grid=(t // tm,),
        in_specs=[pl.BlockSpec((tm, d), lambda i: (i, 0)),
                  pl.BlockSpec((1, d), lambda i: (0, 0)),
                  pl.BlockSpec((d, n), lambda i: (0, 0))],
        out_specs=tuple(pl.BlockSpec((tm, w), lambda i: (i, 0)) for w in widths),
        compiler_params=_cparams(1),
        name="in_proj",
    )(x2, g.reshape(1, d), w_bf)


def _attn_kernel(q_ref, k_ref, v_ref, o_ref, qm_ref, tri_ref, acc_ref, surv_ref,
                 *, head_dim):
    blk = ATT_BLOCK
    i = pl.program_id(1)
    pairs = q_ref.shape[1] // blk
    lane = lax.broadcasted_iota(jnp.int32, (blk, blk), 1)
    sub = lax.broadcasted_iota(jnp.int32, (blk, blk), 0)
    first = lane < head_dim
    for p in range(pairs):
        qp = q_ref[:, p * blk:(p + 1) * blk]
        zero = jnp.zeros_like(qp)
        qm_ref[p, 0:blk, :] = jnp.where(first, qp, zero)
        qm_ref[p, blk:2 * blk, :] = jnp.where(first, zero, qp)
    r2 = lax.broadcasted_iota(jnp.int32, (2 * blk, 2 * blk), 0) & (blk - 1)
    c2 = lax.broadcasted_iota(jnp.int32, (2 * blk, 2 * blk), 1)
    tri_ref[...] = jnp.where((r2 > c2) | (c2 >= blk), 1.0, 0.0).astype(BF16)
    acc_ref[...] = jnp.zeros_like(acc_ref)
    surv_ref[...] = jnp.zeros_like(surv_ref)
    causal = jnp.concatenate([lane < sub, lane < sub], axis=0)

    def block(j, mask):
        start = pl.multiple_of(j * blk, blk)
        log_beta, parts = [], []
        for p in range(pairs):
            kp = k_ref[pl.ds(start, blk), p * blk:(p + 1) * blk]
            z = _dot_nt(qm_ref[p], kp)
            sp = jnp.maximum(z, 0.0) + jnp.log(1.0 + jnp.exp(-jnp.abs(z)))
            log_beta.append(z - sp)
            if mask is not None:
                sp = jnp.where(mask, sp, 0.0)
            hi, lo = _split_bf16(sp)
            parts.append(jnp.concatenate([hi, lo], axis=1))
        sums = _dot(jnp.concatenate(parts, axis=0), tri_ref[...])
        for p in range(pairs):
            rows = slice(p * 2 * blk, (p + 1) * 2 * blk)
            later, total = sums[rows, :blk], sums[rows, blk:]
            w = jnp.exp(log_beta[p] - later - surv_ref[p])
            if mask is not None:
                w = jnp.where(mask, w, 0.0)
            vp = v_ref[pl.ds(start, blk), p * blk:(p + 1) * blk]
            acc_ref[p] += _dot(w.astype(BF16), vp)
            surv_ref[p] += total

    def any_alive():
        return jnp.min(surv_ref[...]) < SURVIVAL_UNDERFLOW

    block(i, causal)

    def cond(carry):
        return (carry[0] < (i >> 1)) & carry[1]

    def body(carry):
        n = carry[0]
        block(i - 1 - 2 * n, None)
        block(i - 2 - 2 * n, None)
        return n + 1, any_alive()

    _, alive = lax.while_loop(cond, body, (jnp.int32(0), any_alive()))

    @pl.when(((i & 1) == 1) & alive)
    def _():
        block(0, None)

    for p in range(pairs):
        o_ref[:, p * blk:(p + 1) * blk] = jnp.where(
            first, acc_ref[p, 0:blk, :], acc_ref[p, blk:2 * blk, :]).astype(o_ref.dtype)


def _attention(q, k, v, batch, seq, head_dim):
    t, width = q.shape
    blk = ATT_BLOCK
    assert 2 * head_dim == blk and width % blk == 0
    pairs = width // blk
    nq = seq // blk
    qspec = pl.BlockSpec((blk, width), lambda b, i: (b * nq + i, 0))
    kvspec = pl.BlockSpec((seq, width), lambda b, i: (b, 0))
    return pl.pallas_call(
        functools.partial(_attn_kernel, head_dim=head_dim),
        out_shape=jax.ShapeDtypeStruct((t, width), BF16),
        grid=(batch, nq),
        in_specs=[qspec, kvspec, kvspec],
        out_specs=qspec,
        scratch_shapes=[pltpu.VMEM((pairs, 2 * blk, blk), BF16),
                        pltpu.VMEM((2 * blk, 2 * blk), BF16),
                        pltpu.VMEM((pairs, 2 * blk, blk), F32),
                        pltpu.VMEM((pairs, 2 * blk, blk), F32)],
        compiler_params=_cparams(2),
        name="attn",
    )(q, k, v)


def _mix_kernel(x_ref, att_ref, u_ref, ga_ref, gp_ref, wa_ref, wp_ref, ps_ref, wo_ref,
                gn_ref, x1_ref, hn_ref, prev_ref, *, tm):
    s = pl.program_id(1)

    @pl.when(s == 0)
    def _():
        prev_ref[...] = jnp.zeros_like(prev_ref)

    pairs, pw, _ = wa_ref.shape
    a_up = jnp.concatenate(
        [_dot(att_ref[:, p * pw:(p + 1) * pw], wa_ref[p]) for p in range(pairs)],
        axis=1)

    u = u_ref[...]
    ext = jnp.concatenate([prev_ref[...], u], axis=0)
    prev_ref[...] = u
    r = lax.broadcasted_iota(jnp.int32, (tm, 2 * tm), 0) + tm
    c = lax.broadcasted_iota(jnp.int32, (tm, 2 * tm), 1)
    pos = (s * tm + lax.broadcasted_iota(jnp.int32, (tm, 1), 0) + 1).astype(F32)
    groups = len(POOL_WINDOWS)
    gdim = u.shape[1] // groups
    uf = u.astype(F32)
    p_parts = []
    for g, w in enumerate(POOL_WINDOWS):
        band = jnp.where((c <= r) & (c > r - w), 1.0, 0.0).astype(BF16)
        sl = slice(g * gdim, (g + 1) * gdim)
        window_sum = _dot(band, ext[:, sl])
        y = window_sum / jnp.minimum(pos, float(w)) - uf[:, sl]
        p_parts.append(_dot(y.astype(BF16), wp_ref[g]))
    p_up = jnp.concatenate(p_parts, axis=1) * ps_ref[...]

    merged = (jax.nn.sigmoid(ga_ref[...].astype(F32)) * a_up
              + jax.nn.sigmoid(gp_ref[...].astype(F32)) * p_up)
    x1 = x_ref[...] + _dot(merged.astype(BF16), wo_ref[...])
    x1_ref[...] = x1
    hn_ref[...] = (_rms(x1) * gn_ref[...]).astype(BF16)


def _pair_block_diag(w):
    h, a, b = w.shape
    z = jnp.zeros((h // 2, a, b), w.dtype)
    top = jnp.concatenate([w[0::2], z], axis=2)
    bottom = jnp.concatenate([z, w[1::2]], axis=2)
    return jnp.concatenate([top, bottom], axis=1)


def _mix(x2, att, u, ga, gp, wa, wp, ps, wo, gn, batch, seq, tm=256):
    t, d = x2.shape
    att_w = att.shape[1]
    pool = u.shape[1]
    ns = seq // tm
    tok = lambda b, s: (b * ns + s, 0)
    const2 = lambda b, s: (0, 0)
    const3 = lambda b, s: (0, 0, 0)
    kern = functools.partial(_mix_kernel, tm=tm)
    return pl.pallas_call(
        kern,
        out_shape=(jax.ShapeDtypeStruct((t, d), F32), jax.ShapeDtypeStruct((t, d), BF16)),
        grid=(batch, ns),
        in_specs=[pl.BlockSpec((tm, d), tok),
                  pl.BlockSpec((tm, att_w), tok),
                  pl.BlockSpec((tm, pool), tok),
                  pl.BlockSpec((tm, d), tok),
                  pl.BlockSpec((tm, d), tok),
                  pl.BlockSpec(wa.shape, const3),
                  pl.BlockSpec(wp.shape, const3),
                  pl.BlockSpec((1, d), const2),
                  pl.BlockSpec((d, d), const2),
                  pl.BlockSpec((1, d), const2)],
        out_specs=(pl.BlockSpec((tm, d), tok), pl.BlockSpec((tm, d), tok)),
        scratch_shapes=[pltpu.VMEM((tm, pool), BF16)],
        compiler_params=_cparams(2),
        name="mix",
    )(x2, att, u, ga, gp, wa, wp, ps.reshape(1, d), wo, gn.reshape(1, d))


def _topk_rows(s, payload, k, pos=None):
    if pos is None:
        pos = lax.broadcasted_iota(jnp.int32, s.shape, 0).astype(F32)
    vals, poss, pays = [], [], []
    for _ in range(k):
        m = jnp.max(s, axis=0, keepdims=True)
        p = jnp.min(jnp.where(s == m, pos, POS_SENTINEL), axis=0, keepdims=True)
        hit = pos == p
        if payload is not None:
            pays.append(jnp.sum(jnp.where(hit, payload, 0.0), axis=0, keepdims=True))
        s = jnp.where(hit, NEG_INF, s)
        vals.append(m)
        poss.append(p)
    cat = lambda xs: jnp.concatenate(xs, axis=0)
    return cat(vals), cat(poss), (cat(pays) if payload is not None else None)


def _route_kernel(hn_ref, wq_ref, keys_ref, idx_ref, gate_ref, *, heads):
    q = _dot(hn_ref[...], wq_ref[...]).astype(BF16)
    half = keys_ref.shape[-1]
    k = PEER_TOPK
    for hd in range(heads):
        tops = []
        for c in range(2):
            lo = (hd * 2 + c) * half
            scores = _dot_nt(keys_ref[hd, c], q[:, lo:lo + half])
            tops.append(_topk_rows(scores, None, k)[:2])
        (s0, i0), (s1, i1) = tops
        h = k // 2
        groups = ([(slice(0, 1), slice(0, k))]
                  + [(slice(a, a + 1), slice(0, h)) for a in range(1, h)]
                  + [(slice(h, k), slice(0, 1))])
        cand_s = jnp.concatenate([s0[ra, :] + s1[rb, :] for ra, rb in groups], axis=0)
        cand_i = jnp.concatenate(
            [i0[ra, :] * float(N_KEYS) + i1[rb, :] for ra, rb in groups], axis=0)
        n = s0.shape[1]
        row = lambda lo, cnt: (lax.broadcasted_iota(jnp.int32, (cnt, n), 0) + lo).astype(F32)
        cand_pos = jnp.concatenate(
            [row(0, k)] + [row(a * k, h) for a in range(1, h)] + [row(h, k - h) * float(k)],
            axis=0)
        best_s, _, best_i = _topk_rows(cand_s, cand_i, k, cand_pos)
        e = jnp.exp(best_s - best_s[0:1, :])
        gate_ref[hd * k:(hd + 1) * k, :] = e / jnp.sum(e, axis=0, keepdims=True)
        idx_ref[hd * k:(hd + 1) * k, :] = best_i.astype(jnp.int32)


def _route(hn, wq_bf, keys_bf, heads, tm=256):
    t, d = hn.shape
    nq = wq_bf.shape[1]
    slots = heads * PEER_TOPK
    kern = functools.partial(_route_kernel, heads=heads)
    return pl.pallas_call(
        kern,
        out_shape=(jax.ShapeDtypeStruct((slots, t), jnp.int32),
                   jax.ShapeDtypeStruct((slots, t), F32)),
        grid=(t // tm,),
        in_specs=[pl.BlockSpec((tm, d), lambda i: (i, 0)),
                  pl.BlockSpec((d, nq), lambda i: (0, 0)),
                  pl.BlockSpec(keys_bf.shape, lambda i: (0, 0, 0, 0))],
        out_specs=(pl.BlockSpec((slots, tm), lambda i: (0, i)),
                   pl.BlockSpec((slots, tm), lambda i: (0, i))),
        compiler_params=_cparams(1),
        name="route",
    )(hn, wq_bf, keys_bf)


def _pack_table(w):
    e, d = w.shape
    bits = lax.bitcast_convert_type(w.astype(BF16), jnp.uint16).astype(jnp.uint32)
    lo, hi = bits[:, :d // 2], bits[:, d // 2:]
    return (lo | (hi << 16)).reshape(e * (d // 256), 128)


def _gather_rows(idx_ref, tab_ref, tile_ref, t, slots, chunks):
    for k in range(slots):
        row = pl.multiple_of(idx_ref.at[k][t], chunks)
        tile_ref[k * chunks:(k + 1) * chunks, :] = tab_ref[pl.ds(row, chunks), :]


def _tile_chunk(tile_ref, c, slots):
    chunks = tile_ref.shape[0] // slots
    return pltpu.bitcast(tile_ref[pl.ds(c, slots, stride=chunks), :], BF16)


def _for_tokens_pipelined(tm, tiles, gather, compute, commit):
    a, b, c, d = tiles
    gather(0, a)
    gather(1, b)
    for t0 in range(0, tm, 4):
        ra = compute(t0, a)
        rb = compute(t0 + 1, b)
        gather(t0 + 2, c)
        gather(t0 + 3, d)
        rc = compute(t0 + 2, c)
        rd = compute(t0 + 3, d)
        if t0 + 4 < tm:
            gather(t0 + 4, a)
            gather(t0 + 5, b)
        commit(t0, (ra, rb, rc, rd))


def _two_tile_step(idx_hbm, idx_smem, sem, process):
    step = pl.program_id(0)

    def index_copy(tile, slot):
        return pltpu.make_async_copy(idx_hbm.at[tile], idx_smem.at[slot], sem.at[slot])

    @pl.when(step == 0)
    def _():
        index_copy(0, 0).start()

    index_copy(2 * step, 0).wait()
    index_copy(2 * step + 1, 1).start()
    process(0, idx_smem.at[0])
    index_copy(2 * step + 1, 1).wait()

    @pl.when(step + 1 < pl.num_programs(0))
    def _():
        index_copy(2 * step + 2, 0).start()

    process(1, idx_smem.at[1])


def _gather_call(kernel_fn, name, idx_rows, table, chunks, blocked_inputs, out_trailing,
                 extra_scratch):
    n_tiles, slots, tm = idx_rows.shape
    t = n_tiles * tm
    def tok(trailing):
        zeros = (0,) * len(trailing)
        return pl.BlockSpec((2 * tm, *trailing), lambda i: (i, *zeros))

    return pl.pallas_call(
        functools.partial(kernel_fn, chunks=chunks),
        out_shape=jax.ShapeDtypeStruct((t, *out_trailing), F32),
        grid=(t // (2 * tm),),
        in_specs=([pl.BlockSpec(memory_space=pl.ANY)]
                  + [tok(x.shape[1:]) for x in blocked_inputs]
                  + [pl.BlockSpec(table.shape, lambda i: (0, 0),
                                  pipeline_mode=pl.Buffered(1))]),
        out_specs=tok(out_trailing),
        scratch_shapes=[pltpu.VMEM((chunks * slots, 128), jnp.uint32)] * 4 + [
                        pltpu.SMEM((2, slots, tm), jnp.int32),
                        pltpu.SemaphoreType.DMA((2,))] + extra_scratch,
        compiler_params=_cparams(1),
        name=name,
    )(idx_rows, *blocked_inputs, table)


def _peer_u_kernel(idx_hbm, hn_ref, gate_ref, tab_ref, w_ref, ta, tb, tc, td, idx_smem, sem,
                   z_ref, *, chunks):
    tm = GATHER_TOKENS
    slots = gate_ref.shape[1]
    half = chunks * 128
    lane = lax.broadcasted_iota(jnp.int32, (2 * slots, 2 * tm), 1)

    def process(part, idx_ref):
        tok = slice(part * tm, (part + 1) * tm)
        hn = hn_ref[tok, :].astype(F32)
        rhs = jnp.concatenate([hn[:, :half], hn[:, half:]], axis=0).T.astype(BF16)
        z_ref[...] = jnp.zeros_like(z_ref)

        def gather(t, tile):
            _gather_rows(idx_ref, tab_ref, tile, t, slots, chunks)

        def compute(t, tile):
            rows = jnp.concatenate([_tile_chunk(tile, c, slots) for c in range(chunks)],
                                   axis=1)
            y = _dot(rows, rhs)
            return jnp.where((lane == t) | (lane == t + tm), y, 0.0)

        def commit(t0, parts):
            z_ref[...] += (parts[0] + parts[1]) + (parts[2] + parts[3])

        _for_tokens_pipelined(tm, (ta, tb, tc, td), gather, compute, commit)
        act_t = (z_ref[pl.ds(0, slots, stride=2), :][:, 0:tm]
                 + z_ref[pl.ds(1, slots, stride=2), :][:, tm:2 * tm])
        pre = act_t.T
        act = 0.5 * pre * (1.0 + lax.erf(pre * math.sqrt(0.5)))
        w_ref[tok, :] = gate_ref[tok, :] * act

    _two_tile_step(idx_hbm, idx_smem, sem, process)


def _peer_u(idx_rows, hn, gates, table, chunks):
    slots = gates.shape[1]
    return _gather_call(_peer_u_kernel, "peer_u", idx_rows, table, chunks, [hn, gates],
                        (slots,), [pltpu.VMEM((2 * slots, 2 * GATHER_TOKENS), F32)])


def _peer_v_kernel(idx_hbm, w_ref, x1_ref, tab_ref, x2_ref, ta, tb, tc, td, idx_smem, sem,
                   wexp_ref, *, chunks):
    tm = GATHER_TOKENS
    slots = w_ref.shape[1]
    rows = 2 * chunks
    group = (2 * LANES) // rows
    q = lax.broadcasted_iota(jnp.int32, (slots, slots * rows), 0)
    col = lax.broadcasted_iota(jnp.int32, (slots, slots * rows), 1)
    expand = jnp.where(col // rows == q, 1.0, 0.0).astype(BF16)
    w_hi, w_lo = _split_bf16(w_ref[...])
    wexp_ref[0] = _dot(w_hi, expand)
    wexp_ref[1] = _dot(w_lo, expand)
    x2_ref[...] = x1_ref[...]
    r = lax.broadcasted_iota(jnp.int32, (rows, 2 * LANES), 0)
    j = lax.broadcasted_iota(jnp.int32, (rows, 2 * LANES), 1) % rows
    keep = jnp.where((j // 2 == r % chunks) & (j % 2 == r // chunks), 1.0, 0.0)

    def process(part, idx_ref):
        def gather(t, tile):
            _gather_rows(idx_ref, tab_ref, tile, t, slots, chunks)

        def compute(t, tile):
            row = part * tm + t
            acc = None
            for g in range(slots // group):
                cols = slice(g * 2 * LANES, (g + 1) * 2 * LANES)
                lhs = jnp.concatenate(
                    [keep * wexp_ref[v, row:row + 1, cols] for v in range(2)],
                    axis=0).astype(BF16)
                rhs = pltpu.bitcast(tile[g * LANES:(g + 1) * LANES, :], BF16)
                res = _dot(lhs, rhs)
                acc = res if acc is None else acc + res
            return acc[:rows] + acc[rows:]

        def commit(t0, parts):
            row = part * tm + t0
            x2_ref[row:row + 4] += jnp.stack(parts)

        _for_tokens_pipelined(tm, (ta, tb, tc, td), gather, compute, commit)

    _two_tile_step(idx_hbm, idx_smem, sem, process)


def _peer_v(idx_rows, w, x1, table, chunks):
    t, d = x1.shape
    slots = w.shape[1]
    rows = 2 * chunks
    x2 = _gather_call(_peer_v_kernel, "peer_v", idx_rows, table, chunks,
                      [w, x1.reshape(t, rows, LANES)], (rows, LANES),
                      [pltpu.VMEM((2, 2 * GATHER_TOKENS, slots * rows), F32)])
    return x2.reshape(t, d)


def _ple_kernel(x_ref, p_ref, gp_ref, wg_ref, wp_ref, gf_ref, o_ref, *, final_norm):
    x = x_ref[...]
    gate = jax.nn.sigmoid(_dot((_rms(x) * gp_ref[...]).astype(BF16), wg_ref[...]))
    x = x + gate * _dot(p_ref[...].astype(BF16), wp_ref[...])
    o_ref[...] = _rms(x) * gf_ref[...] if final_norm else x


def _ple(x2, p2, gp, wg, wp, gf, final_norm, tm=512):
    t, d = x2.shape
    pd = p2.shape[1]
    return pl.pallas_call(
        functools.partial(_ple_kernel, final_norm=final_norm),
        out_shape=jax.ShapeDtypeStruct((t, d), F32),
        grid=(t // tm,),
        in_specs=[pl.BlockSpec((tm, d), lambda i: (i, 0)),
                  pl.BlockSpec((tm, pd), lambda i: (i, 0)),
                  pl.BlockSpec((1, d), lambda i: (0, 0)),
                  pl.BlockSpec((d, d), lambda i: (0, 0)),
                  pl.BlockSpec((pd, d), lambda i: (0, 0)),
                  pl.BlockSpec((1, d), lambda i: (0, 0))],
        out_specs=pl.BlockSpec((tm, d), lambda i: (i, 0)),
        compiler_params=_cparams(1),
        name="ple",
    )(x2, p2, gp.reshape(1, d), wg, wp, gf.reshape(1, d))


def kernel(x, p, norm_mix, w_in, w_att_up, w_pool_group, pool_scale, w_out, norm_ffn,
           w_query, sub_keys, expert_u, expert_v, norm_ple, w_ple_gate, w_ple, norm_final):
    batch, seq, d = x.shape
    depth = w_in.shape[0]
    heads, head_dim = w_att_up.shape[1], w_att_up.shape[2]
    pool = w_pool_group.shape[1] * w_pool_group.shape[2]
    peer_heads = sub_keys.shape[1]
    xt = x.reshape(batch * seq, d)
    for i in range(depth):
        q, k, v, u, ga, gp = _in_proj(xt, norm_mix[i], w_in[i].astype(BF16),
                                      heads * head_dim, head_dim, pool)
        att = _attention(q, k, v, batch, seq, head_dim)
        x1, hn = _mix(xt, att, u, ga, gp, _pair_block_diag(w_att_up[i]).astype(BF16),
                      w_pool_group[i].astype(BF16), pool_scale[i], w_out[i].astype(BF16),
                      norm_ffn[i], batch, seq)
        idx_t, gate_t = _route(hn, w_query[i].astype(BF16), sub_keys[i].astype(BF16),
                               peer_heads)
        chunks = d // 256
        idx_rows = (idx_t * chunks).reshape(idx_t.shape[0], -1, GATHER_TOKENS).swapaxes(0, 1)
        gates = gate_t.T
        w = _peer_u(idx_rows, hn, gates, _pack_table(expert_u[i]), chunks)
        x2 = _peer_v(idx_rows, w, x1, _pack_table(expert_v[i]), chunks)
        xt = _ple(x2, p[i].reshape(batch * seq, -1), norm_ple[i],
                  w_ple_gate[i].astype(BF16), w_ple[i].astype(BF16), norm_final,
                  final_norm=(i == depth - 1))
    return xt.reshape(batch, seq, d)
```
